```python
import jax, jax.numpy as jnp
from jax import lax
import numpy as np

D_MODEL = 1024
BATCH = 4
SEQ = 8192
DEPTH = 4

N_HEADS = 16
N_KV_HEADS = 2
HEAD_DIM = 64
GROUP = N_HEADS // N_KV_HEADS
ROT_DIM = HEAD_DIM // 4
ROPE_THETA = 500000.0
WINDOW = 128
BLOCK = 128
CONV_CH = D_MODEL // 2
CONV_WIDTH = 31
D_FF = -(-(8 * D_MODEL) // (3 * 256)) * 256
EPS = 1e-6
Q_W = N_HEADS * HEAD_DIM
KV_W = N_KV_HEADS * HEAD_DIM
IN_W = Q_W + 2 * KV_W + 2 * CONV_CH + 2 * D_MODEL
SPLITS = tuple(int(s) for s in np.cumsum([Q_W, KV_W, KV_W, CONV_CH, CONV_CH, D_MODEL])[:])

kernel_name = "hybrid_swa_sink_conformer_gated"


def rmsnorm(x, g):
    xf = x.astype(jnp.float32)
    y = xf * lax.rsqrt(jnp.mean(xf * xf, axis=-1, keepdims=True) + EPS)
    return (y * g.astype(jnp.float32)).astype(x.dtype)


def layernorm(x, g, b):
    xf = x.astype(jnp.float32)
    mu = jnp.mean(xf, axis=-1, keepdims=True)
    var = jnp.mean(jnp.square(xf - mu), axis=-1, keepdims=True)
    y = (xf - mu) * lax.rsqrt(var + EPS)
    return (y * g.astype(jnp.float32) + b.astype(jnp.float32)).astype(x.dtype)


def rope_tables(seq):
    inv_freq = ROPE_THETA ** (-jnp.arange(0, ROT_DIM, 2, dtype=jnp.float32) / ROT_DIM)
    ang = jnp.arange(seq, dtype=jnp.float32)[:, None] * inv_freq[None, :]
    return jnp.cos(ang), jnp.sin(ang)


def partial_rope(x, cos, sin):
    half = ROT_DIM // 2
    c = cos[None, :, None, :].astype(x.dtype)
    s = sin[None, :, None, :].astype(x.dtype)
    x1, x2, xp = x[..., :half], x[..., half:ROT_DIM], x[..., ROT_DIM:]
    return jnp.concatenate([x1 * c - x2 * s, x2 * c + x1 * s, xp], axis=-1)


def sliding_window_attention(q, k, v, sinks):
    B, T = q.shape[0], q.shape[1]
    nb = T // BLOCK
    qb = q.reshape(B, nb, BLOCK, N_KV_HEADS, GROUP, HEAD_DIM)

    def band(t):
        tp = jnp.pad(t, ((0, 0), (BLOCK, 0), (0, 0), (0, 0)))
        tp = tp.reshape(B, nb + 1, BLOCK, N_KV_HEADS, HEAD_DIM)
        return jnp.concatenate([tp[:, :-1], tp[:, 1:]], axis=2)

    kb, vb = band(k), band(v)
    scale = HEAD_DIM ** -0.5
    s = jnp.einsum('bnqkgd,bnskd->bnkgqs', qb, kb,
                   preferred_element_type=jnp.float32) * scale
    qi = jnp.arange(BLOCK)[:, None]
    sj = jnp.arange(2 * BLOCK)[None, :]
    rel = qi + BLOCK - sj
    kpos = jnp.arange(nb)[:, None, None] * BLOCK - BLOCK + sj
    mask = (rel >= 0) & (rel < WINDOW) & (kpos >= 0)
    s = jnp.where(mask[None, :, None, None], s, -jnp.inf)
    sink = sinks.astype(jnp.float32).reshape(N_KV_HEADS, GROUP)[None, None, :, :, None, None]
    m = jnp.maximum(jnp.max(s, axis=-1, keepdims=True), sink)
    p = jnp.exp(s - m)
    p = p / (jnp.sum(p, axis=-1, keepdims=True) + jnp.exp(sink - m))
    o = jnp.einsum('bnkgqs,bnskd->bnqkgd', p.astype(v.dtype), vb)
    return o.reshape(B, T, N_HEADS * HEAD_DIM)


def conformer_conv(u, ug, w_dw, b_dw, ln_g, ln_b, w_pw):
    a = u * jax.nn.sigmoid(ug)
    y = lax.conv_general_dilated(
        a, w_dw[:, None, :].astype(a.dtype), window_strides=(1,),
        padding=[(CONV_WIDTH - 1, 0)],
        dimension_numbers=('NWC', 'WIO', 'NWC'),
        feature_group_count=CONV_CH) + b_dw.astype(a.dtype)
    y = layernorm(y, ln_g, ln_b)
    y = jax.nn.silu(y)
    return jnp.einsum('btc,cd->btd', y, w_pw)


def setup_inputs(seed: int = 0) -> dict:
    key = jax.random.key(seed)
    ks = jax.random.split(key, 16)
    f32 = jnp.float32

    def nrm(k, shape, scale):
        return jax.random.normal(k, shape, f32) * scale

    return {
        "x": nrm(ks[0], (BATCH, SEQ, D_MODEL), 1.0),
        "norm_mix": 1.0 + nrm(ks[1], (DEPTH, D_MODEL), 0.02),
        "w_in": nrm(ks[2], (DEPTH, D_MODEL, IN_W), D_MODEL ** -0.5),
        "q_norm": 1.0 + nrm(ks[3], (DEPTH, HEAD_DIM), 0.02),
        "k_norm": 1.0 + nrm(ks[4], (DEPTH, HEAD_DIM), 0.02),
        "sinks": nrm(ks[5], (DEPTH, N_HEADS), 0.5),
        "conv_w": nrm(ks[6], (DEPTH, CONV_WIDTH, CONV_CH), CONV_WIDTH ** -0.5),
        "conv_b": nrm(ks[7], (DEPTH, CONV_CH), 0.02),
        "conv_ln_g": 1.0 + nrm(ks[8], (DEPTH, CONV_CH), 0.02),
        "conv_ln_b": nrm(ks[9], (DEPTH, CONV_CH), 0.02),
        "w_conv_out": nrm(ks[10], (DEPTH, CONV_CH, D_MODEL), CONV_CH ** -0.5),
        "w_out": nrm(ks[11], (DEPTH, D_MODEL, D_MODEL), D_MODEL ** -0.5),
        "norm_ffn": 1.0 + nrm(ks[12], (DEPTH, D_MODEL), 0.02),
        "w_gate_up": nrm(ks[13], (DEPTH, D_MODEL, 2 * D_FF), D_MODEL ** -0.5),
        "w_down": nrm(ks[14], (DEPTH, D_FF, D_MODEL), D_FF ** -0.5),
    }


def reference(x, norm_mix, w_in, q_norm, k_norm, sinks, conv_w, conv_b, conv_ln_g,
              conv_ln_b, w_conv_out, w_out, norm_ffn, w_gate_up, w_down):
    B, T = x.shape[0], x.shape[1]
    cos, sin = rope_tables(T)
    for l in range(DEPTH):
        h = rmsnorm(x, norm_mix[l])
        proj = jnp.einsum('btd,de->bte', h, w_in[l])
        q, k, v, u, ug, ga, gb = jnp.split(proj, SPLITS, axis=-1)
        q = q.reshape(B, T, N_HEADS, HEAD_DIM)
        k = k.reshape(B, T, N_KV_HEADS, HEAD_DIM)
        v = v.reshape(B, T, N_KV_HEADS, HEAD_DIM)
        q = partial_rope(rmsnorm(q, q_norm[l]), cos, sin)
        k = partial_rope(rmsnorm(k, k_norm[l]), cos, sin)
        a_out = sliding_window_attention(q, k, v, sinks[l])
        c_out = conformer_conv(u, ug, conv_w[l], conv_b[l], conv_ln_g[l],
                               conv_ln_b[l], w_conv_out[l])
        merged = jax.nn.sigmoid(ga) * a_out + jax.nn.sigmoid(gb) * c_out
        x = x + jnp.einsum('btd,de->bte', merged, w_out[l])
        h2 = rmsnorm(x, norm_ffn[l])
        gu = jnp.einsum('btd,df->btf', h2, w_gate_up[l])
        g, up = jnp.split(gu, 2, axis=-1)
        x = x + jnp.einsum('btf,fd->btd', jax.nn.silu(g) * up, w_down[l])
    return x
```

```python
import functools

import jax
import jax.numpy as jnp
import numpy as np
from jax import lax
from jax.experimental import pallas as pl
from jax.experimental.pallas import tpu as pltpu

D_MODEL = 1024
N_HEADS = 16
N_KV_HEADS = 2
HEAD_DIM = 64
GROUP = N_HEADS // N_KV_HEADS
ROT_DIM = HEAD_DIM // 4
ROPE_THETA = 500000.0
WINDOW = 128
CONV_CH = D_MODEL // 2
CONV_WIDTH = 31
D_FF = -(-(8 * D_MODEL) // (3 * 256)) * 256
EPS = 1e-6
Q_W = N_HEADS * HEAD_DIM
KV_W = N_KV_HEADS * HEAD_DIM
IN_W = Q_W + 2 * KV_W + 2 * CONV_CH + 2 * D_MODEL

LANES = 128
HALO = 32
PAIRS = GROUP // 2
VMEM_LIMIT = 56 * 1024 * 1024

TM_PROJ = 256
TQ = WINDOW
TM_FFN = 256

_F32 = jnp.float32
_BF16 = jnp.bfloat16


def _seg64_allsum(x):
    lane = lax.broadcasted_iota(jnp.int32, x.shape, 1)
    for sh in (32, 16, 8, 4, 2, 1):
        up = pltpu.roll(x, sh, 1)
        dn = pltpu.roll(x, LANES - sh, 1)
        x = x + jnp.where((lane & sh) != 0, up, dn)
    return x


def _head_norm_rope(xg, w, c, s1, s2):
    ss = _seg64_allsum(xg * xg)
    xn = xg * lax.rsqrt(ss * (1.0 / HEAD_DIM) + EPS) * w
    return xn * c + pltpu.roll(xn, ROT_DIM // 2, 1) * s1 + pltpu.roll(xn, LANES - ROT_DIM // 2, 1) * s2


def _proj_kernel(x_ref, g_ref, w_ref, qw_ref, kw_ref, c_ref, s1_ref, s2_ref,
                 q_out, k_out, v_out, a_out, ga_out, gb_out):
    x = x_ref[0]
    ms = jnp.mean(x * x, axis=-1, keepdims=True)
    h = (x * lax.rsqrt(ms + EPS) * g_ref[...]).astype(_BF16)
    c, s1, s2 = c_ref[...], s1_ref[...], s2_ref[...]

    q = jnp.dot(h, w_ref[:, :Q_W], preferred_element_type=_F32)
    qw = qw_ref[...]
    scale = HEAD_DIM ** -0.5
    for g in range(Q_W // LANES):
        sl = slice(g * LANES, (g + 1) * LANES)
        q_out[0, :, sl] = (_head_norm_rope(q[:, sl], qw, c, s1, s2) * scale).astype(_BF16)

    kv = jnp.dot(h, w_ref[:, Q_W:Q_W + 2 * KV_W], preferred_element_type=_F32)
    k_out[0] = _head_norm_rope(kv[:, :KV_W], kw_ref[...], c, s1, s2).astype(_BF16)
    v_out[0] = kv[:, KV_W:].astype(_BF16)

    o = Q_W + 2 * KV_W
    uu = jnp.dot(h, w_ref[:, o:o + 2 * CONV_CH], preferred_element_type=_F32)
    a_out[0] = uu[:, :CONV_CH] * jax.nn.sigmoid(uu[:, CONV_CH:])

    o += 2 * CONV_CH
    gates = jnp.dot(h, w_ref[:, o:o + 2 * D_MODEL], preferred_element_type=_F32)
    ga_out[0] = jax.nn.sigmoid(gates[:, :D_MODEL]).astype(_BF16)
    gb_out[0] = jax.nn.sigmoid(gates[:, D_MODEL:]).astype(_BF16)


def _mix_kernel(sinks_ref, x_ref, q_ref, kc_ref, kp_ref, vc_ref, vp_ref, ac_ref, ah_ref,
                ga_ref, gb_ref, cw_ref, cb_ref, lg_ref, lb_ref, wpw_ref, wout_ref,
                o_ref, aext_ref):
    i = pl.program_id(1)
    has_prev = i > 0

    q = q_ref[0]
    kband = jnp.concatenate([kc_ref[0], kp_ref[0]], axis=0)
    vband = jnp.concatenate([vc_ref[0], vp_ref[0]], axis=0)
    lane = lax.broadcasted_iota(jnp.int32, kband.shape, 1)
    low = lane < HEAD_DIM
    kswap = pltpu.roll(kband, HEAD_DIM, 1)
    vswap = pltpu.roll(vband, HEAD_DIM, 1)
    zero = jnp.zeros_like(kband)

    rows = 2 * PAIRS * TQ
    qi = lax.broadcasted_iota(jnp.int32, (rows, TQ), 0) & (TQ - 1)
    kj = lax.broadcasted_iota(jnp.int32, (rows, TQ), 1)
    tri = kj <= qi
    valid = jnp.logical_or(tri, has_prev)
    lane_o = lax.broadcasted_iota(jnp.int32, (PAIRS * TQ, LANES), 1)

    att = []
    for g in range(N_KV_HEADS):
        src_k, src_v = (kband, vband) if g == 0 else (kswap, vswap)
        oth_k, oth_v = (kswap, vswap) if g == 0 else (kband, vband)
        k_even = jnp.where(low, src_k, zero)
        k_odd = jnp.where(low, zero, oth_k)
        v_even = jnp.where(low, src_v, zero)
        v_odd = jnp.where(low, zero, oth_v)
        base = g * GROUP * HEAD_DIM
        qp = jnp.concatenate([q[:, base + t * LANES: base + (t + 1) * LANES] for t in range(PAIRS)], axis=0)
        dn = (((1,), (1,)), ((), ()))
        s = jnp.concatenate([lax.dot_general(qp, k_even, dn, preferred_element_type=_F32),
                             lax.dot_general(qp, k_odd, dn, preferred_element_type=_F32)], axis=0)
        sc = jnp.where(tri, s[:, :TQ], s[:, TQ:])
        sc = jnp.where(valid, sc, -jnp.inf)
        sink = jnp.concatenate(
            [jnp.full((TQ, 1), sinks_ref[g * GROUP + 2 * t + par], _F32)
             for par in range(2) for t in range(PAIRS)], axis=0)
        m = jnp.maximum(jnp.max(sc, axis=-1, keepdims=True), sink)
        p = jnp.exp(sc - m)
        den = jnp.sum(p, axis=-1, keepdims=True) + jnp.exp(sink - m)
        pz = jnp.zeros_like(p)
        pb = jnp.concatenate([jnp.where(tri, p, pz), jnp.where(tri, pz, p)], axis=1).astype(_BF16)
        half = PAIRS * TQ
        o = (jnp.dot(pb[:half], v_even, preferred_element_type=_F32)
             + jnp.dot(pb[half:], v_odd, preferred_element_type=_F32))
        inv = 1.0 / den
        o = o * jnp.where(lane_o < HEAD_DIM, inv[:half], inv[half:])
        att.extend(o[t * TQ:(t + 1) * TQ] for t in range(PAIRS))
    a_att = jnp.concatenate(att, axis=1)

    aext_ref[:HALO] = jnp.where(has_prev, ah_ref[0], jnp.zeros_like(ah_ref[0]))
    aext_ref[HALO:] = ac_ref[0]
    y = jnp.broadcast_to(cb_ref[...], (TQ, CONV_CH))
    for j in range(CONV_WIDTH):
        off = HALO - (CONV_WIDTH - 1) + j
        y = y + cw_ref[j:j + 1, :] * aext_ref[off:off + TQ, :]
    mu = jnp.mean(y, axis=-1, keepdims=True)
    yc = y - mu
    var = jnp.mean(yc * yc, axis=-1, keepdims=True)
    yn = yc * lax.rsqrt(var + EPS) * lg_ref[...] + lb_ref[...]
    act = (yn * jax.nn.sigmoid(yn)).astype(_BF16)
    c_out = jnp.dot(act, wpw_ref[...], preferred_element_type=_F32)

    merged = ga_ref[0].astype(_F32) * a_att + gb_ref[0].astype(_F32) * c_out
    o_ref[0] = x_ref[0] + jnp.dot(merged.astype(_BF16), wout_ref[...], preferred_element_type=_F32)


def _ffn_kernel(x_ref, g_ref, wgu_ref, wd_ref, o_ref):
    x = x_ref[0]
    ms = jnp.mean(x * x, axis=-1, keepdims=True)
    h = (x * lax.rsqrt(ms + EPS) * g_ref[...]).astype(_BF16)
    gate = jnp.dot(h, wgu_ref[:, :D_FF], preferred_element_type=_F32)
    up = jnp.dot(h, wgu_ref[:, D_FF:], preferred_element_type=_F32)
    act = (gate * jax.nn.sigmoid(gate) * up).astype(_BF16)
    o_ref[0] = x + jnp.dot(act, wd_ref[...], preferred_element_type=_F32)


def _resident(shape):
    return pl.BlockSpec(shape, lambda *_: (0,) * len(shape), pipeline_mode=pl.Buffered(1))


def _params():
    return pltpu.CompilerParams(dimension_semantics=("parallel", "parallel"), vmem_limit_bytes=VMEM_LIMIT)


def _proj_call(x, g, w, qw, kw, c, s1, s2):
    B, T, _ = x.shape
    row = lambda width: pl.BlockSpec((1, TM_PROJ, width), lambda b, i: (b, i, 0))
    tab = pl.BlockSpec((TM_PROJ, LANES), lambda b, i: (i, 0))
    out = lambda width, dt: jax.ShapeDtypeStruct((B, T, width), dt)
    return pl.pallas_call(
        _proj_kernel,
        grid=(B, T // TM_PROJ),
        in_specs=[row(D_MODEL), _resident((1, D_MODEL)), _resident((D_MODEL, IN_W)),
                  _resident((1, LANES)), _resident((1, LANES)), tab, tab, tab],
        out_specs=[row(Q_W), row(KV_W), row(KV_W), row(CONV_CH), row(D_MODEL), row(D_MODEL)],
        out_shape=[out(Q_W, _BF16), out(KV_W, _BF16), out(KV_W, _BF16), out(CONV_CH, _F32),
                   out(D_MODEL, _BF16), out(D_MODEL, _BF16)],
        compiler_params=_params(),
        name="proj",
    )(x, g, w, qw, kw, c, s1, s2)


def _mix_call(sinks, x, q, k, v, a, ga, gb, cw, cb, lg, lb, wpw, wout):
    B, T, _ = x.shape
    row = lambda width: pl.BlockSpec((1, TQ, width), lambda b, i: (b, i, 0))
    prev = pl.BlockSpec((1, TQ, KV_W), lambda b, i: (b, jnp.maximum(i - 1, 0), 0))
    halo = pl.BlockSpec((1, HALO, CONV_CH), lambda b, i: (b, jnp.maximum(i * (TQ // HALO) - 1, 0), 0))
    return pl.pallas_call(
        _mix_kernel,
        grid=(B, T // TQ),
        in_specs=[pl.BlockSpec(memory_space=pltpu.SMEM),
                  row(D_MODEL), row(Q_W), row(KV_W), prev, row(KV_W), prev, row(CONV_CH), halo,
                  row(D_MODEL), row(D_MODEL),
                  _resident((CONV_WIDTH, CONV_CH)), _resident((1, CONV_CH)), _resident((1, CONV_CH)),
                  _resident((1, CONV_CH)), _resident((CONV_CH, D_MODEL)), _resident((D_MODEL, D_MODEL))],
        out_specs=row(D_MODEL),
        out_shape=jax.ShapeDtypeStruct((B, T, D_MODEL), _F32),
        scratch_shapes=[pltpu.VMEM((HALO + TQ, CONV_CH), _F32)],
        compiler_params=_params(),
        name="mix",
    )(sinks, x, q, k, k, v, v, a, a, ga, gb, cw, cb, lg, lb, wpw, wout)


def _ffn_call(x, g, wgu, wd):
    B, T, _ = x.shape
    row = pl.BlockSpec((1, TM_FFN, D_MODEL), lambda b, i: (b, i, 0))
    return pl.pallas_call(
        _ffn_kernel,
        grid=(B, T // TM_FFN),
        in_specs=[row, _resident((1, D_MODEL)), _resident((D_MODEL, 2 * D_FF)), _resident((D_FF, D_MODEL))],
        out_specs=row,
        out_shape=jax.ShapeDtypeStruct((B, T, D_MODEL), _F32),
        compiler_params=_params(),
        name="ffn",
    )(x, g, wgu, wd)


def _rope_lane_tables(seq):
    half = ROT_DIM // 2
    inv_freq = ROPE_THETA ** (-jnp.arange(0, ROT_DIM, 2, dtype=_F32) / ROT_DIM)
    ang = jnp.arange(seq, dtype=_F32)[:, None] * inv_freq[None, :]
    cos, sin = jnp.cos(ang), jnp.sin(ang)
    ones = jnp.ones((seq, HEAD_DIM - ROT_DIM), _F32)
    zeros = jnp.zeros((seq, HEAD_DIM - ROT_DIM), _F32)
    zh = jnp.zeros((seq, half), _F32)
    c = jnp.concatenate([cos, cos, ones], axis=1)
    s1 = jnp.concatenate([zh, sin, zeros], axis=1)
    s2 = jnp.concatenate([-sin, zh, zeros], axis=1)
    dup = lambda t: jnp.concatenate([t, t], axis=1)
    return dup(c), dup(s1), dup(s2)


def kernel(x, norm_mix, w_in, q_norm, k_norm, sinks, conv_w, conv_b, conv_ln_g, conv_ln_b,
           w_conv_out, w_out, norm_ffn, w_gate_up, w_down):
    B, T, D = x.shape
    depth = w_in.shape[0]
    assert D == D_MODEL and T % TM_PROJ == 0 and T % TQ == 0 and T % TM_FFN == 0
    c, s1, s2 = _rope_lane_tables(T)
    dup = lambda t: jnp.concatenate([t, t], axis=-1)[None, :]
    for l in range(depth):
        q, k, v, a, ga, gb = _proj_call(
            x, norm_mix[l][None, :], w_in[l].astype(_BF16), dup(q_norm[l]), dup(k_norm[l]), c, s1, s2)
        x = _mix_call(sinks[l], x, q, k, v, a, ga, gb, conv_w[l], conv_b[l][None, :],
                      conv_ln_g[l][None, :], conv_ln_b[l][None, :],
                      w_conv_out[l].astype(_BF16), w_out[l].astype(_BF16))
        x = _ffn_call(x, norm_ffn[l][None, :], w_gate_up[l].astype(_BF16), w_down[l].astype(_BF16))
    return x
```

```python
import math

import jax
import jax.numpy as jnp
from jax import lax
from jax.experimental import pallas as pl
from jax.experimental.pallas import tpu as pltpu

D_MODEL = 1024
N_HEADS = 16
N_KV_HEADS = 2
HEAD_DIM = 64
GROUP = N_HEADS // N_KV_HEADS
ROT_DIM = HEAD_DIM // 4
ROPE_THETA = 500000.0
WINDOW = 128
CONV_CH = D_MODEL // 2
CONV_WIDTH = 31
D_FF = -(-(8 * D_MODEL) // (3 * 256)) * 256
EPS = 1e-6
Q_W = N_HEADS * HEAD_DIM
KV_W = N_KV_HEADS * HEAD_DIM
IN_W = Q_W + 2 * KV_W + 2 * CONV_CH + 2 * D_MODEL

LANES = 128
SUBLANES = 8
MXU_DIM = 256
HALO = 32
PAIRS = GROUP // 2
VMEM_LIMIT = 56 * 1024 * 1024
LOG2E = math.log2(math.e)

TM_PROJ = 512
NSUB = 2
TM_MIX = NSUB * WINDOW
TM_FFN = 512
CONV_LEAD = HALO - (CONV_WIDTH - 1)
SHIFT_ROWS = TM_MIX + HALO - SUBLANES

_F32 = jnp.float32
_BF16 = jnp.bfloat16


def _segment_sumsq(x, ones_blockdiag):
    xx = x * x
    hi = xx.astype(_BF16)
    lo = (xx - hi.astype(_F32)).astype(_BF16)
    width = ones_blockdiag.shape[0]
    parts = []
    for c in range(x.shape[1] // width):
        sl = slice(c * width, (c + 1) * width)
        parts.append(jnp.dot(hi[:, sl], ones_blockdiag, preferred_element_type=_F32)
                     + jnp.dot(lo[:, sl], ones_blockdiag, preferred_element_type=_F32))
    return parts[0] if len(parts) == 1 else jnp.concatenate(parts, axis=1)


def _norm_rope(xg, ss, w, c, s1, s2):
    xn = xg * lax.rsqrt(ss * (1.0 / HEAD_DIM) + EPS) * w
    return xn * c + pltpu.roll(xn, ROT_DIM // 2, 1) * s1 + pltpu.roll(xn, LANES - ROT_DIM // 2, 1) * s2


def _proj_kernel(x_ref, g_ref, w_ref, qw_ref, kw_ref, c_ref, s1_ref, s2_ref, ones_ref,
                 q_out, k_out, v_out, a_out, ga_out, gb_out):
    x = x_ref[0]
    ms = jnp.mean(x * x, axis=-1, keepdims=True)
    h = (x * lax.rsqrt(ms + EPS) * g_ref[...]).astype(_BF16)
    c, s1, s2 = c_ref[...], s1_ref[...], s2_ref[...]

    q = jnp.dot(h, w_ref[:, :Q_W], preferred_element_type=_F32)
    q_ss = _segment_sumsq(q, ones_ref[...])
    qw = qw_ref[...]
    scale = HEAD_DIM ** -0.5 * LOG2E
    for g in range(Q_W // LANES):
        sl = slice(g * LANES, (g + 1) * LANES)
        q_out[0, :, sl] = (_norm_rope(q[:, sl], q_ss[:, sl], qw, c, s1, s2) * scale).astype(_BF16)

    kv = jnp.dot(h, w_ref[:, Q_W:Q_W + 2 * KV_W], preferred_element_type=_F32)
    k = kv[:, :KV_W]
    k_ss = _segment_sumsq(k, ones_ref[:KV_W, :KV_W])
    k_out[0] = _norm_rope(k, k_ss, kw_ref[...], c, s1, s2).astype(_BF16)
    v_out[0] = kv[:, KV_W:].astype(_BF16)

    o = Q_W + 2 * KV_W
    uu = jnp.dot(h, w_ref[:, o:o + 2 * CONV_CH], preferred_element_type=_F32)
    a_out[0] = uu[:, :CONV_CH] * jax.nn.sigmoid(uu[:, CONV_CH:])

    o += 2 * CONV_CH
    gates = jnp.dot(h, w_ref[:, o:o + 2 * D_MODEL], preferred_element_type=_F32)
    ga_out[0] = jax.nn.sigmoid(gates[:, :D_MODEL]).astype(_BF16)
    gb_out[0] = jax.nn.sigmoid(gates[:, D_MODEL:]).astype(_BF16)


def _attend_block(q, kband, vband, sink_of, first_block):
    tq = WINDOW
    lane = lax.broadcasted_iota(jnp.int32, kband.shape, 1)
    low = lane < HEAD_DIM
    kswap = pltpu.roll(kband, HEAD_DIM, 1)
    vswap = pltpu.roll(vband, HEAD_DIM, 1)
    zero = jnp.zeros_like(kband)
    ones_lo = low.astype(_F32).astype(_BF16)
    ones_hi = (1.0 - low.astype(_F32)).astype(_BF16)

    rows = 2 * PAIRS * tq
    half = PAIRS * tq
    qi = lax.broadcasted_iota(jnp.int32, (rows, tq), 0) & (tq - 1)
    kj = lax.broadcasted_iota(jnp.int32, (rows, tq), 1)
    tri = kj <= qi
    low_o = lax.broadcasted_iota(jnp.int32, (half, LANES), 1) < HEAD_DIM
    contract_last = (((1,), (1,)), ((), ()))

    att = []
    for g in range(N_KV_HEADS):
        src_k, src_v = (kband, vband) if g == 0 else (kswap, vswap)
        oth_k, oth_v = (kswap, vswap) if g == 0 else (kband, vband)
        k_even = jnp.where(low, src_k, zero)
        k_odd = jnp.where(low, zero, oth_k)
        w_even = jnp.concatenate([jnp.where(low, src_v, zero), ones_lo], axis=1)
        w_odd = jnp.concatenate([jnp.where(low, zero, oth_v), ones_hi], axis=1)
        base = g * GROUP * HEAD_DIM
        qp = jnp.concatenate([q[:, base + t * LANES: base + (t + 1) * LANES] for t in range(PAIRS)], axis=0)
        s = jnp.concatenate([lax.dot_general(qp, k_even, contract_last, preferred_element_type=_F32),
                             lax.dot_general(qp, k_odd, contract_last, preferred_element_type=_F32)], axis=0)
        sc = jnp.where(tri, s[:, :tq], s[:, tq:])
        if first_block is not None:
            sc = jnp.where(jnp.logical_or(tri, jnp.logical_not(first_block)), sc, -jnp.inf)
        sink = jnp.concatenate(
            [jnp.full((tq, 1), sink_of(g * GROUP + 2 * t + par), _F32)
             for par in range(2) for t in range(PAIRS)], axis=0)
        m = jnp.maximum(jnp.max(sc, axis=-1, keepdims=True), sink)
        p = jnp.exp2(sc - m)
        sink_p = jnp.exp2(sink - m)
        pz = jnp.zeros_like(p)
        pb = jnp.concatenate([jnp.where(tri, p, pz), jnp.where(tri, pz, p)], axis=1).astype(_BF16)
        wide = (jnp.dot(pb[:half], w_even, preferred_element_type=_F32)
                + jnp.dot(pb[half:], w_odd, preferred_element_type=_F32))
        den = wide[:, LANES:] + jnp.where(low_o, sink_p[:half], sink_p[half:])
        o = wide[:, :LANES] / den
        att.extend(o[t * tq:(t + 1) * tq] for t in range(PAIRS))
    return jnp.concatenate(att, axis=1)


def _mix_kernel(sinks_ref, x_ref, q_ref, kc_ref, kp_ref, vc_ref, vp_ref, ac_ref, ah_ref,
                ga_ref, gb_ref, cw_ref, cb_ref, lg_ref, lb_ref, wpw_ref, wout_ref,
                o_ref, aext_ref, shift_ref):
    first = pl.program_id(1) == 0

    k_all = jnp.concatenate([kp_ref[0], kc_ref[0]], axis=0)
    v_all = jnp.concatenate([vp_ref[0], vc_ref[0]], axis=0)
    sink_of = lambda head: sinks_ref[head] * LOG2E
    blocks = []
    for j in range(NSUB):
        cur = slice((j + 1) * WINDOW, (j + 2) * WINDOW)
        prv = slice(j * WINDOW, (j + 1) * WINDOW)
        kband = jnp.concatenate([k_all[cur], k_all[prv]], axis=0)
        vband = jnp.concatenate([v_all[cur], v_all[prv]], axis=0)
        blocks.append(_attend_block(q_ref[0, j * WINDOW:(j + 1) * WINDOW, :], kband, vband, sink_of,
                                    first if j == 0 else None))
    a_att = jnp.concatenate(blocks, axis=0)

    aext_ref[:HALO] = jnp.where(first, jnp.zeros_like(ah_ref[0]), ah_ref[0])
    aext_ref[HALO:] = ac_ref[0]
    for r in range(1, SUBLANES):
        shift_ref[r - 1] = aext_ref[r:r + SHIFT_ROWS, :]
    y = jnp.broadcast_to(cb_ref[...], (TM_MIX, CONV_CH))
    for j in range(CONV_WIDTH):
        off = CONV_LEAD + j
        r, base = off % SUBLANES, off - off % SUBLANES
        win = aext_ref[base:base + TM_MIX, :] if r == 0 else shift_ref[r - 1, base:base + TM_MIX, :]
        y = y + cw_ref[j:j + 1, :] * win
    mu = jnp.mean(y, axis=-1, keepdims=True)
    yc = y - mu
    var = jnp.mean(yc * yc, axis=-1, keepdims=True)
    yn = yc * lax.rsqrt(var + EPS) * lg_ref[...] + lb_ref[...]
    act = (yn * jax.nn.sigmoid(yn)).astype(_BF16)
    c_out = jnp.dot(act, wpw_ref[...], preferred_element_type=_F32)

    merged = ga_ref[0].astype(_F32) * a_att + gb_ref[0].astype(_F32) * c_out
    o_ref[0] = x_ref[0] + jnp.dot(merged.astype(_BF16), wout_ref[...], preferred_element_type=_F32)


def _ffn_kernel(x_ref, g_ref, wgu_ref, wd_ref, o_ref):
    x = x_ref[0]
    ms = jnp.mean(x * x, axis=-1, keepdims=True)
    h = (x * lax.rsqrt(ms + EPS) * g_ref[...]).astype(_BF16)
    gate = jnp.dot(h, wgu_ref[:, :D_FF], preferred_element_type=_F32)
    up = jnp.dot(h, wgu_ref[:, D_FF:], preferred_element_type=_F32)
    act = (gate * jax.nn.sigmoid(gate) * up).astype(_BF16)
    o_ref[0] = x + jnp.dot(act, wd_ref[...], preferred_element_type=_F32)


def _resident(shape):
    return pl.BlockSpec(shape, lambda *_: (0,) * len(shape), pipeline_mode=pl.Buffered(1))


def _params():
    return pltpu.CompilerParams(dimension_semantics=("parallel", "parallel"), vmem_limit_bytes=VMEM_LIMIT)


def _proj_call(x, g, w, qw, kw, c, s1, s2, ones_blockdiag):
    B, T, _ = x.shape
    row = lambda width: pl.BlockSpec((1, TM_PROJ, width), lambda b, i: (b, i, 0))
    tab = pl.BlockSpec((TM_PROJ, LANES), lambda b, i: (i, 0))
    out = lambda width, dt: jax.ShapeDtypeStruct((B, T, width), dt)
    return pl.pallas_call(
        _proj_kernel,
        grid=(B, T // TM_PROJ),
        in_specs=[row(D_MODEL), _resident((1, D_MODEL)), _resident((D_MODEL, IN_W)),
                  _resident((1, LANES)), _resident((1, LANES)), tab, tab, tab,
                  _resident((MXU_DIM, MXU_DIM))],
        out_specs=[row(Q_W), row(KV_W), row(KV_W), row(CONV_CH), row(D_MODEL), row(D_MODEL)],
        out_shape=[out(Q_W, _BF16), out(KV_W, _BF16), out(KV_W, _BF16), out(CONV_CH, _F32),
                   out(D_MODEL, _BF16), out(D_MODEL, _BF16)],
        compiler_params=_params(),
        name="proj",
    )(x, g, w, qw, kw, c, s1, s2, ones_blockdiag)


def _mix_call(sinks, x, q, k, v, a, ga, gb, cw, cb, lg, lb, wpw, wout):
    B, T, _ = x.shape
    row = lambda width: pl.BlockSpec((1, TM_MIX, width), lambda b, i: (b, i, 0))
    prev = pl.BlockSpec((1, WINDOW, KV_W), lambda b, i: (b, jnp.maximum(i * NSUB - 1, 0), 0))
    halo = pl.BlockSpec((1, HALO, CONV_CH), lambda b, i: (b, jnp.maximum(i * (TM_MIX // HALO) - 1, 0), 0))
    return pl.pallas_call(
        _mix_kernel,
        grid=(B, T // TM_MIX),
        in_specs=[pl.BlockSpec(memory_space=pltpu.SMEM),
                  row(D_MODEL), row(Q_W), row(KV_W), prev, row(KV_W), prev, row(CONV_CH), halo,
                  row(D_MODEL), row(D_MODEL),
                  _resident((CONV_WIDTH, CONV_CH)), _resident((1, CONV_CH)), _resident((1, CONV_CH)),
                  _resident((1, CONV_CH)), _resident((CONV_CH, D_MODEL)), _resident((D_MODEL, D_MODEL))],
        out_specs=row(D_MODEL),
        out_shape=jax.ShapeDtypeStruct((B, T, D_MODEL), _F32),
        scratch_shapes=[pltpu.VMEM((HALO + TM_MIX, CONV_CH), _F32),
                        pltpu.VMEM((SUBLANES - 1, SHIFT_ROWS, CONV_CH), _F32)],
        compiler_params=_params(),
        name="mix",
    )(sinks, x, q, k, k, v, v, a, a, ga, gb, cw, cb, lg, lb, wpw, wout)


def _ffn_call(x, g, wgu, wd):
    B, T, _ = x.shape
    row = pl.BlockSpec((1, TM_FFN, D_MODEL), lambda b, i: (b, i, 0))
    return pl.pallas_call(
        _ffn_kernel,
        grid=(B, T // TM_FFN),
        in_specs=[row, _resident((1, D_MODEL)), _resident((D_MODEL, 2 * D_FF)), _resident((D_FF, D_MODEL))],
        out_specs=row,
        out_shape=jax.ShapeDtypeStruct((B, T, D_MODEL), _F32),
        compiler_params=_params(),
        name="ffn",
    )(x, g, wgu, wd)


def _rope_lane_tables(seq):
    half = ROT_DIM // 2
    inv_freq = ROPE_THETA ** (-jnp.arange(0, ROT_DIM, 2, dtype=_F32) / ROT_DIM)
    ang = jnp.arange(seq, dtype=_F32)[:, None] * inv_freq[None, :]
    cos, sin = jnp.cos(ang), jnp.sin(ang)
    ones = jnp.ones((seq, HEAD_DIM - ROT_DIM), _F32)
    zeros = jnp.zeros((seq, HEAD_DIM - ROT_DIM), _F32)
    zh = jnp.zeros((seq, half), _F32)
    c = jnp.concatenate([cos, cos, ones], axis=1)
    s1 = jnp.concatenate([zh, sin, zeros], axis=1)
    s2 = jnp.concatenate([-sin, zh, zeros], axis=1)
    dup = lambda t: jnp.concatenate([t, t], axis=1)
    return dup(c), dup(s1), dup(s2)


def kernel(x, norm_mix, w_in, q_norm, k_norm, sinks, conv_w, conv_b, conv_ln_g, conv_ln_b,
           w_conv_out, w_out, norm_ffn, w_gate_up, w_down):
    B, T, D = x.shape
    depth = w_in.shape[0]
    assert D == D_MODEL and T % TM_PROJ == 0 and T % TM_MIX == 0 and T % TM_FFN == 0
    c, s1, s2 = _rope_lane_tables(T)
    seg = jnp.arange(MXU_DIM) // HEAD_DIM
    ones_blockdiag = (seg[:, None] == seg[None, :]).astype(_BF16)
    dup = lambda t: jnp.concatenate([t, t], axis=-1)[None, :]
    for l in range(depth):
        q, k, v, a, ga, gb = _proj_call(
            x, norm_mix[l][None, :], w_in[l].astype(_BF16), dup(q_norm[l]), dup(k_norm[l]), c, s1, s2,
            ones_blockdiag)
        x = _mix_call(sinks[l], x, q, k, v, a, ga, gb, conv_w[l], conv_b[l][None, :],
                      conv_ln_g[l][None, :], conv_ln_b[l][None, :],
                      w_conv_out[l].astype(_BF16), w_out[l].astype(_BF16))
        x = _ffn_call(x, norm_ffn[l][None, :], w_gate_up[l].astype(_BF16), w_down[l].astype(_BF16))
    return x
```

```python
import functools
import math

import jax
import jax.numpy as jnp
from jax import lax
from jax.experimental import pallas as pl
from jax.experimental.pallas import tpu as pltpu

D_MODEL = 1024
N_HEADS = 16
N_KV_HEADS = 2
HEAD_DIM = 64
GROUP = N_HEADS // N_KV_HEADS
ROT_DIM = HEAD_DIM // 4
ROPE_THETA = 500000.0
WINDOW = 128
CONV_CH = D_MODEL // 2
CONV_WIDTH = 31
D_FF = -(-(8 * D_MODEL) // (3 * 256)) * 256
EPS = 1e-6
Q_W = N_HEADS * HEAD_DIM
KV_W = N_KV_HEADS * HEAD_DIM
IN_W = Q_W + 2 * KV_W + 2 * CONV_CH + 2 * D_MODEL

LANES = 128
SUBLANES = 8
MXU_DIM = 256
HALO = 32
PAIRS = GROUP // 2
VMEM_LIMIT = 56 * 1024 * 1024
LOG2E = math.log2(math.e)

TM_PROJ = 512
NSUB = 2
TM_MIX = NSUB * WINDOW
FFN_CHUNK = MXU_DIM
CONV_LEAD = HALO - (CONV_WIDTH - 1)
SHIFT_ROWS = TM_MIX + HALO - SUBLANES

_F32 = jnp.float32
_BF16 = jnp.bfloat16


def _segment_sumsq(x, ones_blockdiag):
    xx = x * x
    hi = xx.astype(_BF16)
    lo = (xx - hi.astype(_F32)).astype(_BF16)
    width = ones_blockdiag.shape[0]
    parts = []
    for c in range(x.shape[1] // width):
        sl = slice(c * width, (c + 1) * width)
        parts.append(jnp.dot(hi[:, sl], ones_blockdiag, preferred_element_type=_F32)
                     + jnp.dot(lo[:, sl], ones_blockdiag, preferred_element_type=_F32))
    return parts[0] if len(parts) == 1 else jnp.concatenate(parts, axis=1)


def _norm_rope(xg, ss, w, c, s1, s2):
    xn = xg * lax.rsqrt(ss * (1.0 / HEAD_DIM) + EPS) * w
    return xn * c + pltpu.roll(xn, ROT_DIM // 2, 1) * s1 + pltpu.roll(xn, LANES - ROT_DIM // 2, 1) * s2


def _proj_kernel(x_ref, g_ref, w_ref, qw_ref, kw_ref, c_ref, s1_ref, s2_ref, ones_ref,
                 q_out, k_out, v_out, a_out, ga_out, gb_out):
    x = x_ref[0]
    ms = jnp.mean(x * x, axis=-1, keepdims=True)
    h = (x * lax.rsqrt(ms + EPS) * g_ref[...]).astype(_BF16)
    c, s1, s2 = c_ref[...], s1_ref[...], s2_ref[...]

    q = jnp.dot(h, w_ref[:, :Q_W], preferred_element_type=_F32)
    q_ss = _segment_sumsq(q, ones_ref[...])
    qw = qw_ref[...]
    scale = HEAD_DIM ** -0.5 * LOG2E
    for g in range(Q_W // LANES):
        sl = slice(g * LANES, (g + 1) * LANES)
        q_out[0, :, sl] = (_norm_rope(q[:, sl], q_ss[:, sl], qw, c, s1, s2) * scale).astype(_BF16)

    kv = jnp.dot(h, w_ref[:, Q_W:Q_W + 2 * KV_W], preferred_element_type=_F32)
    k = kv[:, :KV_W]
    k_ss = _segment_sumsq(k, ones_ref[:KV_W, :KV_W])
    k_out[0] = _norm_rope(k, k_ss, kw_ref[...], c, s1, s2).astype(_BF16)
    v_out[0] = kv[:, KV_W:].astype(_BF16)

    o = Q_W + 2 * KV_W
    uu = jnp.dot(h, w_ref[:, o:o + 2 * CONV_CH], preferred_element_type=_F32)
    a_out[0] = uu[:, :CONV_CH] * jax.nn.sigmoid(uu[:, CONV_CH:])

    o += 2 * CONV_CH
    gates = jnp.dot(h, w_ref[:, o:o + 2 * D_MODEL], preferred_element_type=_F32)
    ga_out[0] = jax.nn.sigmoid(gates[:, :D_MODEL]).astype(_BF16)
    gb_out[0] = jax.nn.sigmoid(gates[:, D_MODEL:]).astype(_BF16)


def _attend_prep(kband, vband):
    lane = lax.broadcasted_iota(jnp.int32, kband.shape, 1)
    low = lane < HEAD_DIM
    zero = jnp.zeros_like(kband)
    rows = 2 * PAIRS * WINDOW
    qi = lax.broadcasted_iota(jnp.int32, (rows, WINDOW), 0) & (WINDOW - 1)
    kj = lax.broadcasted_iota(jnp.int32, (rows, WINDOW), 1)
    return dict(
        k=(kband, pltpu.roll(kband, HEAD_DIM, 1)), v=(vband, pltpu.roll(vband, HEAD_DIM, 1)),
        low=low, zero=zero,
        ones_lo=low.astype(_F32).astype(_BF16), ones_hi=(1.0 - low.astype(_F32)).astype(_BF16),
        tri=kj <= qi,
        low_o=lax.broadcasted_iota(jnp.int32, (PAIRS * WINDOW, LANES), 1) < HEAD_DIM)


def _attend_scores(q, prep, g):
    low, zero = prep["low"], prep["zero"]
    k_even = jnp.where(low, prep["k"][g], zero)
    k_odd = jnp.where(low, zero, prep["k"][1 - g])
    base = g * GROUP * HEAD_DIM
    qp = jnp.concatenate([q[:, base + t * LANES: base + (t + 1) * LANES] for t in range(PAIRS)], axis=0)
    contract_last = (((1,), (1,)), ((), ()))
    return jnp.concatenate([lax.dot_general(qp, k_even, contract_last, preferred_element_type=_F32),
                            lax.dot_general(qp, k_odd, contract_last, preferred_element_type=_F32)], axis=0)


def _attend_softmax(s, prep, g, sink_of, first_block):
    tri = prep["tri"]
    sc = jnp.where(tri, s[:, :WINDOW], s[:, WINDOW:])
    if first_block is not None:
        sc = jnp.where(jnp.logical_or(tri, jnp.logical_not(first_block)), sc, -jnp.inf)
    sink = jnp.concatenate(
        [jnp.full((WINDOW, 1), sink_of(g * GROUP + 2 * t + par), _F32)
         for par in range(2) for t in range(PAIRS)], axis=0)
    m = jnp.maximum(jnp.max(sc, axis=-1, keepdims=True), sink)
    p = jnp.exp2(sc - m)
    pz = jnp.zeros_like(p)
    pb = jnp.concatenate([jnp.where(tri, p, pz), jnp.where(tri, pz, p)], axis=1).astype(_BF16)
    return pb, jnp.exp2(sink - m)


def _attend_values(pb, sink_p, prep, g):
    low, zero = prep["low"], prep["zero"]
    half = PAIRS * WINDOW
    w_even = jnp.concatenate([jnp.where(low, prep["v"][g], zero), prep["ones_lo"]], axis=1)
    w_odd = jnp.concatenate([jnp.where(low, zero, prep["v"][1 - g]), prep["ones_hi"]], axis=1)
    wide = (jnp.dot(pb[:half], w_even, preferred_element_type=_F32)
            + jnp.dot(pb[half:], w_odd, preferred_element_type=_F32))
    den = wide[:, LANES:] + jnp.where(prep["low_o"], sink_p[:half], sink_p[half:])
    o = wide[:, :LANES] / den
    return [o[t * WINDOW:(t + 1) * WINDOW] for t in range(PAIRS)]


def _mix_ffn_kernel(sinks_ref, x_ref, q_ref, kc_ref, kp_ref, vc_ref, vp_ref, ac_ref, ah_ref,
                    ga_ref, gb_ref, cw_ref, cb_ref, lg_ref, lb_ref, wpw_ref, wout_ref,
                    gf_ref, wgu_ref, wd_ref,
                    o_ref, aext_ref, shift_ref, x1_ref, *, blocks_per_seq, n_blocks):
    s = pl.program_id(0)
    first = jnp.minimum(s, n_blocks - 1) % blocks_per_seq == 0

    @pl.when(s == 0)
    def _():
        x1_ref[...] = jnp.zeros_like(x1_ref)

    x1 = x1_ref[...]
    ms = jnp.mean(x1 * x1, axis=-1, keepdims=True)
    h = (x1 * lax.rsqrt(ms + EPS) * gf_ref[...]).astype(_BF16)
    ffn = {"acc": x1, "next": 0}

    def ffn_chunks(n, after=None):
        for _ in range(n):
            c = ffn["next"]
            if c >= D_FF // FFN_CHUNK:
                return
            hc = h
            if after is not None:
                parts = [after[i:i + SUBLANES] for i in range(0, after.shape[0], SUBLANES)]
                while len(parts) > 1:
                    parts = [a + b for a, b in zip(parts[::2], parts[1::2])] + parts[len(parts) & ~1:]
                u = pltpu.bitcast(parts[0], jnp.uint32)
                z = lax.shift_right_logical(lax.shift_right_logical(u, jnp.uint32(16)), jnp.uint32(16))
                ninf = pltpu.bitcast(z | jnp.uint32(0xFF800000), _F32)
                ninf = jnp.tile(ninf, (2, 1)).astype(_BF16)
                hc = jnp.maximum(h, jnp.tile(ninf, (TM_MIX // (2 * SUBLANES), D_MODEL // LANES)))
            cols = slice(c * FFN_CHUNK, (c + 1) * FFN_CHUNK)
            gate = jnp.dot(hc, wgu_ref[:, cols], preferred_element_type=_F32)
            up = jnp.dot(hc, wgu_ref[:, D_FF + c * FFN_CHUNK:D_FF + (c + 1) * FFN_CHUNK],
                         preferred_element_type=_F32)
            act = (gate * jax.nn.sigmoid(gate) * up).astype(_BF16)
            ffn["acc"] = ffn["acc"] + jnp.dot(act, wd_ref[cols, :], preferred_element_type=_F32)
            ffn["next"] = c + 1

    aext_ref[:HALO] = jnp.where(first, jnp.zeros_like(ah_ref[...]), ah_ref[...])
    aext_ref[HALO:] = ac_ref[...]
    for r in range(1, SUBLANES):
        shift_ref[r - 1] = aext_ref[r:r + SHIFT_ROWS, :]

    y_groups = []
    for grp in range(CONV_CH // LANES):
        lanes = slice(grp * LANES, (grp + 1) * LANES)
        y = jnp.broadcast_to(cb_ref[:, lanes], (TM_MIX, LANES))
        for j in range(CONV_WIDTH):
            off = CONV_LEAD + j
            r, base = off % SUBLANES, off - off % SUBLANES
            win = (aext_ref[base:base + TM_MIX, lanes] if r == 0
                   else shift_ref[r - 1, base:base + TM_MIX, lanes])
            y = y + cw_ref[j:j + 1, lanes] * win
        y_groups.append(y)
    y = jnp.concatenate(y_groups, axis=1)
    mu = jnp.mean(y, axis=-1, keepdims=True)
    yc = y - mu
    var = jnp.mean(yc * yc, axis=-1, keepdims=True)
    yn = yc * lax.rsqrt(var + EPS) * lg_ref[...] + lb_ref[...]
    cact = (yn * jax.nn.sigmoid(yn)).astype(_BF16)
    c_out = jnp.dot(cact, wpw_ref[...], preferred_element_type=_F32)

    k_all = jnp.concatenate([kp_ref[...], kc_ref[...]], axis=0)
    v_all = jnp.concatenate([vp_ref[...], vc_ref[...]], axis=0)
    sink_of = lambda head: sinks_ref[head] * LOG2E
    blocks = []
    for j in range(NSUB):
        cur = slice((j + 1) * WINDOW, (j + 2) * WINDOW)
        prv = slice(j * WINDOW, (j + 1) * WINDOW)
        prep = _attend_prep(jnp.concatenate([k_all[cur], k_all[prv]], axis=0),
                            jnp.concatenate([v_all[cur], v_all[prv]], axis=0))
        q = q_ref[j * WINDOW:(j + 1) * WINDOW, :]
        slabs = []
        for g in range(N_KV_HEADS):
            sc = _attend_scores(q, prep, g)
            pb, sink_p = _attend_softmax(sc, prep, g, sink_of, first if j == 0 else None)
            slabs.extend(_attend_values(pb, sink_p, prep, g))
        blocks.append(jnp.concatenate(slabs, axis=1))
    a_att = jnp.concatenate(blocks, axis=0)

    merged = ga_ref[...].astype(_F32) * a_att + gb_ref[...].astype(_F32) * c_out
    x1_new = x_ref[...] + jnp.dot(merged.astype(_BF16), wout_ref[...], preferred_element_type=_F32)
    ffn_chunks(D_FF // FFN_CHUNK)
    o_ref[...] = ffn["acc"]
    x1_ref[...] = x1_new


def _resident(shape):
    return pl.BlockSpec(shape, lambda *_: (0,) * len(shape), pipeline_mode=pl.Buffered(1))


def _params():
    return pltpu.CompilerParams(dimension_semantics=("parallel", "parallel"), vmem_limit_bytes=VMEM_LIMIT)


def _proj_call(x, g, w, qw, kw, c, s1, s2, ones_blockdiag):
    B, T, _ = x.shape
    row = lambda width: pl.BlockSpec((1, TM_PROJ, width), lambda b, i: (b, i, 0))
    tab = pl.BlockSpec((TM_PROJ, LANES), lambda b, i: (i, 0))
    out = lambda width, dt: jax.ShapeDtypeStruct((B, T, width), dt)
    return pl.pallas_call(
        _proj_kernel,
        grid=(B, T // TM_PROJ),
        in_specs=[row(D_MODEL), _resident((1, D_MODEL)), _resident((D_MODEL, IN_W)),
                  _resident((1, LANES)), _resident((1, LANES)), tab, tab, tab,
                  _resident((MXU_DIM, MXU_DIM))],
        out_specs=[row(Q_W), row(KV_W), row(KV_W), row(CONV_CH), row(D_MODEL), row(D_MODEL)],
        out_shape=[out(Q_W, _BF16), out(KV_W, _BF16), out(KV_W, _BF16), out(CONV_CH, _F32),
                   out(D_MODEL, _BF16), out(D_MODEL, _BF16)],
        compiler_params=_params(),
        name="proj",
    )(x, g, w, qw, kw, c, s1, s2, ones_blockdiag)


def _mix_ffn_call(sinks, x, q, k, v, a, ga, gb, cw, cb, lg, lb, wpw, wout, gf, wgu, wd):
    B, T, _ = x.shape
    flat = lambda t: t.reshape(B * T, t.shape[-1])
    n_blocks = B * T // TM_MIX
    blk = lambda s: jnp.minimum(s, n_blocks - 1)
    row = lambda width: pl.BlockSpec((TM_MIX, width), lambda s: (blk(s), 0))
    prev = pl.BlockSpec((WINDOW, KV_W), lambda s: (jnp.maximum(blk(s) * NSUB - 1, 0), 0))
    halo = pl.BlockSpec((HALO, CONV_CH), lambda s: (jnp.maximum(blk(s) * (TM_MIX // HALO) - 1, 0), 0))
    body = functools.partial(_mix_ffn_kernel, blocks_per_seq=T // TM_MIX, n_blocks=n_blocks)
    out = pl.pallas_call(
        body,
        grid=(n_blocks + 1,),
        in_specs=[pl.BlockSpec(memory_space=pltpu.SMEM),
                  row(D_MODEL), row(Q_W), row(KV_W), prev, row(KV_W), prev, row(CONV_CH), halo,
                  row(D_MODEL), row(D_MODEL),
                  _resident((CONV_WIDTH, CONV_CH)), _resident((1, CONV_CH)), _resident((1, CONV_CH)),
                  _resident((1, CONV_CH)), _resident((CONV_CH, D_MODEL)), _resident((D_MODEL, D_MODEL)),
                  _resident((1, D_MODEL)), _resident((D_MODEL, 2 * D_FF)), _resident((D_FF, D_MODEL))],
        out_specs=pl.BlockSpec((TM_MIX, D_MODEL), lambda s: (jnp.maximum(s - 1, 0), 0)),
        out_shape=jax.ShapeDtypeStruct((B * T, D_MODEL), _F32),
        scratch_shapes=[pltpu.VMEM((HALO + TM_MIX, CONV_CH), _F32),
                        pltpu.VMEM((SUBLANES - 1, SHIFT_ROWS, CONV_CH), _F32),
                        pltpu.VMEM((TM_MIX, D_MODEL), _F32)],
        compiler_params=pltpu.CompilerParams(dimension_semantics=("arbitrary",), vmem_limit_bytes=VMEM_LIMIT),
        name="mix_ffn",
    )(sinks, flat(x), flat(q), flat(k), flat(k), flat(v), flat(v), flat(a), flat(a), flat(ga), flat(gb),
      cw, cb, lg, lb, wpw, wout, gf, wgu, wd)
    return out.reshape(B, T, D_MODEL)


def _rope_lane_tables(seq):
    half = ROT_DIM // 2
    inv_freq = ROPE_THETA ** (-jnp.arange(0, ROT_DIM, 2, dtype=_F32) / ROT_DIM)
    ang = jnp.arange(seq, dtype=_F32)[:, None] * inv_freq[None, :]
    cos, sin = jnp.cos(ang), jnp.sin(ang)
    ones = jnp.ones((seq, HEAD_DIM - ROT_DIM), _F32)
    zeros = jnp.zeros((seq, HEAD_DIM - ROT_DIM), _F32)
    zh = jnp.zeros((seq, half), _F32)
    c = jnp.concatenate([cos, cos, ones], axis=1)
    s1 = jnp.concatenate([zh, sin, zeros], axis=1)
    s2 = jnp.concatenate([-sin, zh, zeros], axis=1)
    dup = lambda t: jnp.concatenate([t, t], axis=1)
    return dup(c), dup(s1), dup(s2)


def kernel(x, norm_mix, w_in, q_norm, k_norm, sinks, conv_w, conv_b, conv_ln_g, conv_ln_b,
           w_conv_out, w_out, norm_ffn, w_gate_up, w_down):
    B, T, D = x.shape
    depth = w_in.shape[0]
    assert D == D_MODEL and T % TM_PROJ == 0 and T % TM_MIX == 0
    c, s1, s2 = _rope_lane_tables(T)
    seg = jnp.arange(MXU_DIM) // HEAD_DIM
    ones_blockdiag = (seg[:, None] == seg[None, :]).astype(_BF16)
    dup = lambda t: jnp.concatenate([t, t], axis=-1)[None, :]
    for l in range(depth):
        q, k, v, a, ga, gb = _proj_call(
            x, norm_mix[l][None, :], w_in[l].astype(_BF16), dup(q_norm[l]), dup(k_norm[l]), c, s1, s2,
            ones_blockdiag)
        x = _mix_ffn_call(sinks[l], x, q, k, v, a, ga, gb, conv_w[l], conv_b[l][None, :],
                          conv_ln_g[l][None, :], conv_ln_b[l][None, :],
                          w_conv_out[l].astype(_BF16), w_out[l].astype(_BF16),
                          norm_ffn[l][None, :], w_gate_up[l].astype(_BF16), w_down[l].astype(_BF16))
    return x
```

```python
import functools
import math

import jax
import jax.numpy as jnp
from jax import lax
from jax.experimental import pallas as pl
from jax.experimental.pallas import tpu as pltpu

D_MODEL = 1024
N_HEADS = 16
N_KV_HEADS = 2
HEAD_DIM = 64
GROUP = N_HEADS // N_KV_HEADS
ROT_DIM = HEAD_DIM // 4
ROPE_THETA = 500000.0
WINDOW = 128
CONV_CH = D_MODEL // 2
CONV_WIDTH = 31
D_FF = -(-(8 * D_MODEL) // (3 * 256)) * 256
EPS = 1e-6
Q_W = N_HEADS * HEAD_DIM
KV_W = N_KV_HEADS * HEAD_DIM
IN_W = Q_W + 2 * KV_W + 2 * CONV_CH + 2 * D_MODEL

LANES = 128
SUBLANES = 8
MXU_DIM = 256
HALO = 32
PAIRS = GROUP // 2
VMEM_LIMIT = 56 * 1024 * 1024
LOG2E = math.log2(math.e)

TM_PROJ = 512
NSUB = 2
TM_MIX = NSUB * WINDOW
CONV_LEAD = HALO - (CONV_WIDTH - 1)
SHIFT_ROWS = TM_MIX + HALO - SUBLANES

_F32 = jnp.float32
_BF16 = jnp.bfloat16


def _segment_sumsq(x, ones_blockdiag):
    xx = x * x
    hi = xx.astype(_BF16)
    lo = (xx - hi.astype(_F32)).astype(_BF16)
    width = ones_blockdiag.shape[0]
    parts = []
    for c in range(x.shape[1] // width):
        sl = slice(c * width, (c + 1) * width)
        parts.append(jnp.dot(hi[:, sl], ones_blockdiag, preferred_element_type=_F32)
                     + jnp.dot(lo[:, sl], ones_blockdiag, preferred_element_type=_F32))
    return parts[0] if len(parts) == 1 else jnp.concatenate(parts, axis=1)


def _norm_rope(xg, ss, w, c, s1, s2):
    xn = xg * lax.rsqrt(ss * (1.0 / HEAD_DIM) + EPS) * w
    return xn * c + pltpu.roll(xn, ROT_DIM // 2, 1) * s1 + pltpu.roll(xn, LANES - ROT_DIM // 2, 1) * s2


def _proj_kernel(x_ref, g_ref, w_ref, qw_ref, kw_ref, c_ref, s1_ref, s2_ref, ones_ref,
                 q_out, k_out, v_out, a_out, ga_out, gb_out):
    x = x_ref[0]
    ms = jnp.mean(x * x, axis=-1, keepdims=True)
    h = (x * lax.rsqrt(ms + EPS) * g_ref[...]).astype(_BF16)
    c, s1, s2 = c_ref[...], s1_ref[...], s2_ref[...]

    q = jnp.dot(h, w_ref[:, :Q_W], preferred_element_type=_F32)
    q_ss = _segment_sumsq(q, ones_ref[...])
    qw = qw_ref[...]
    scale = HEAD_DIM ** -0.5 * LOG2E
    for g in range(Q_W // LANES):
        sl = slice(g * LANES, (g + 1) * LANES)
        q_out[0, :, sl] = (_norm_rope(q[:, sl], q_ss[:, sl], qw, c, s1, s2) * scale).astype(_BF16)

    kv = jnp.dot(h, w_ref[:, Q_W:Q_W + 2 * KV_W], preferred_element_type=_F32)
    k = kv[:, :KV_W]
    k_ss = _segment_sumsq(k, ones_ref[:KV_W, :KV_W])
    k_out[0] = _norm_rope(k, k_ss, kw_ref[...], c, s1, s2).astype(_BF16)
    v_out[0] = kv[:, KV_W:].astype(_BF16)

    o = Q_W + 2 * KV_W
    uu = jnp.dot(h, w_ref[:, o:o + 2 * CONV_CH], preferred_element_type=_F32)
    a_out[0] = uu[:, :CONV_CH] * jax.nn.sigmoid(uu[:, CONV_CH:])

    o += 2 * CONV_CH
    gates = jnp.dot(h, w_ref[:, o:o + 2 * D_MODEL], preferred_element_type=_F32)
    ga_out[0] = jax.nn.sigmoid(gates[:, :D_MODEL]).astype(_BF16)
    gb_out[0] = jax.nn.sigmoid(gates[:, D_MODEL:]).astype(_BF16)


def _attend_prep(kband, vband):
    lane = lax.broadcasted_iota(jnp.int32, kband.shape, 1)
    low = lane < HEAD_DIM
    zero = jnp.zeros_like(kband)
    rows = 2 * PAIRS * WINDOW
    qi = lax.broadcasted_iota(jnp.int32, (rows, WINDOW), 0) & (WINDOW - 1)
    kj = lax.broadcasted_iota(jnp.int32, (rows, WINDOW), 1)
    return dict(
        k=(kband, pltpu.roll(kband, HEAD_DIM, 1)), v=(vband, pltpu.roll(vband, HEAD_DIM, 1)),
        low=low, zero=zero,
        ones_lo=low.astype(_F32).astype(_BF16), ones_hi=(1.0 - low.astype(_F32)).astype(_BF16),
        tri=kj <= qi,
        low_o=lax.broadcasted_iota(jnp.int32, (PAIRS * WINDOW, LANES), 1) < HEAD_DIM)


def _attend_scores(q, prep, g):
    low, zero = prep["low"], prep["zero"]
    k_even = jnp.where(low, prep["k"][g], zero)
    k_odd = jnp.where(low, zero, prep["k"][1 - g])
    base = g * GROUP * HEAD_DIM
    qp = jnp.concatenate([q[:, base + t * LANES: base + (t + 1) * LANES] for t in range(PAIRS)], axis=0)
    contract_last = (((1,), (1,)), ((), ()))
    return jnp.concatenate([lax.dot_general(qp, k_even, contract_last, preferred_element_type=_F32),
                            lax.dot_general(qp, k_odd, contract_last, preferred_element_type=_F32)], axis=0)


def _attend_softmax(s, prep, g, sink_of, first_block):
    tri = prep["tri"]
    sc = jnp.where(tri, s[:, :WINDOW], s[:, WINDOW:])
    if first_block is not None:
        sc = jnp.where(jnp.logical_or(tri, jnp.logical_not(first_block)), sc, -jnp.inf)
    sink = jnp.concatenate(
        [jnp.full((WINDOW, 1), sink_of(g * GROUP + 2 * t + par), _F32)
         for par in range(2) for t in range(PAIRS)], axis=0)
    m = jnp.maximum(jnp.max(sc, axis=-1, keepdims=True), sink)
    p = jnp.exp2(sc - m)
    pz = jnp.zeros_like(p)
    pb = jnp.concatenate([jnp.where(tri, p, pz), jnp.where(tri, pz, p)], axis=1).astype(_BF16)
    return pb, jnp.exp2(sink - m)


def _attend_values(pb, sink_p, prep, g):
    low, zero = prep["low"], prep["zero"]
    half = PAIRS * WINDOW
    w_even = jnp.concatenate([jnp.where(low, prep["v"][g], zero), prep["ones_lo"]], axis=1)
    w_odd = jnp.concatenate([jnp.where(low, zero, prep["v"][1 - g]), prep["ones_hi"]], axis=1)
    wide = (jnp.dot(pb[:half], w_even, preferred_element_type=_F32)
            + jnp.dot(pb[half:], w_odd, preferred_element_type=_F32))
    den = wide[:, LANES:] + jnp.where(prep["low_o"], sink_p[:half], sink_p[half:])
    o = wide[:, :LANES] / den
    return [o[t * WINDOW:(t + 1) * WINDOW] for t in range(PAIRS)]


def _mix_ffn_kernel(sinks_ref, x_ref, q_ref, kc_ref, kp_ref, vc_ref, vp_ref, ac_ref, ah_ref,
                    ga_ref, gb_ref, cw_ref, cb_ref, lg_ref, lb_ref, wpw_ref, wout_ref,
                    gf_ref, wgu_ref, wd_ref,
                    o_ref, aext_ref, shift_ref, x1_ref, *, blocks_per_seq, n_blocks):
    s = pl.program_id(0)
    first = jnp.minimum(s, n_blocks - 1) % blocks_per_seq == 0

    @pl.when(s == 0)
    def _():
        x1_ref[...] = jnp.zeros_like(x1_ref)

    k_all = jnp.concatenate([kp_ref[...], kc_ref[...]], axis=0)
    v_all = jnp.concatenate([vp_ref[...], vc_ref[...]], axis=0)
    sink_of = lambda head: sinks_ref[head] * LOG2E
    blocks = []
    for j in range(NSUB):
        cur = slice((j + 1) * WINDOW, (j + 2) * WINDOW)
        prv = slice(j * WINDOW, (j + 1) * WINDOW)
        prep = _attend_prep(jnp.concatenate([k_all[cur], k_all[prv]], axis=0),
                            jnp.concatenate([v_all[cur], v_all[prv]], axis=0))
        q = q_ref[j * WINDOW:(j + 1) * WINDOW, :]
        slabs = []
        for g in range(N_KV_HEADS):
            sc = _attend_scores(q, prep, g)
            pb, sink_p = _attend_softmax(sc, prep, g, sink_of, first if j == 0 else None)
            slabs.extend(_attend_values(pb, sink_p, prep, g))
        blocks.append(jnp.concatenate(slabs, axis=1))
    a_att = jnp.concatenate(blocks, axis=0)

    aext_ref[:HALO] = jnp.where(first, jnp.zeros_like(ah_ref[...]), ah_ref[...])
    aext_ref[HALO:] = ac_ref[...]
    for r in range(1, SUBLANES):
        shift_ref[r - 1] = aext_ref[r:r + SHIFT_ROWS, :]
    y = jnp.broadcast_to(cb_ref[...], (TM_MIX, CONV_CH))
    for j in range(CONV_WIDTH):
        off = CONV_LEAD + j
        r, base = off % SUBLANES, off - off % SUBLANES
        win = aext_ref[base:base + TM_MIX, :] if r == 0 else shift_ref[r - 1, base:base + TM_MIX, :]
        y = y + cw_ref[j:j + 1, :] * win
    mu = jnp.mean(y, axis=-1, keepdims=True)
    yc = y - mu
    var = jnp.mean(yc * yc, axis=-1, keepdims=True)
    yn = yc * lax.rsqrt(var + EPS) * lg_ref[...] + lb_ref[...]
    cact = (yn * jax.nn.sigmoid(yn)).astype(_BF16)
    c_out = jnp.dot(cact, wpw_ref[...], preferred_element_type=_F32)

    merged = ga_ref[...].astype(_F32) * a_att + gb_ref[...].astype(_F32) * c_out
    x1_new = x_ref[...] + jnp.dot(merged.astype(_BF16), wout_ref[...], preferred_element_type=_F32)

    x1 = x1_ref[...]
    ms = jnp.mean(x1 * x1, axis=-1, keepdims=True)
    h = (x1 * lax.rsqrt(ms + EPS) * gf_ref[...]).astype(_BF16)
    gate = jnp.dot(h, wgu_ref[:, :D_FF], preferred_element_type=_F32)
    up = jnp.dot(h, wgu_ref[:, D_FF:], preferred_element_type=_F32)
    act = (gate * jax.nn.sigmoid(gate) * up).astype(_BF16)
    o_ref[...] = x1_ref[...] + jnp.dot(act, wd_ref[...], preferred_element_type=_F32)

    x1_ref[...] = x1_new


def _resident(shape):
    return pl.BlockSpec(shape, lambda *_: (0,) * len(shape), pipeline_mode=pl.Buffered(1))


def _resident_layer(shape, layer):
    return pl.BlockSpec((None,) + shape, lambda *_: (layer,) + (0,) * len(shape), pipeline_mode=pl.Buffered(1))


def _params():
    return pltpu.CompilerParams(dimension_semantics=("parallel", "parallel"), vmem_limit_bytes=VMEM_LIMIT)


def _proj_call(layer, x, g, w, qw, kw, c, s1, s2, ones_blockdiag):
    B, T, _ = x.shape
    row = lambda width: pl.BlockSpec((1, TM_PROJ, width), lambda b, i: (b, i, 0))
    tab = pl.BlockSpec((TM_PROJ, LANES), lambda b, i: (i, 0))
    out = lambda width, dt: jax.ShapeDtypeStruct((B, T, width), dt)
    return pl.pallas_call(
        _proj_kernel,
        grid=(B, T // TM_PROJ),
        in_specs=[row(D_MODEL), _resident((1, D_MODEL)), _resident_layer((D_MODEL, IN_W), layer),
                  _resident((1, LANES)), _resident((1, LANES)), tab, tab, tab,
                  _resident((MXU_DIM, MXU_DIM))],
        out_specs=[row(Q_W), row(KV_W), row(KV_W), row(CONV_CH), row(D_MODEL), row(D_MODEL)],
        out_shape=[out(Q_W, _BF16), out(KV_W, _BF16), out(KV_W, _BF16), out(CONV_CH, _F32),
                   out(D_MODEL, _BF16), out(D_MODEL, _BF16)],
        compiler_params=_params(),
        name="proj",
    )(x, g, w, qw, kw, c, s1, s2, ones_blockdiag)


def _mix_ffn_call(layer, sinks, x, q, k, v, a, ga, gb, cw, cb, lg, lb, wpw, wout, gf, wgu, wd):
    B, T, _ = x.shape
    flat = lambda t: t.reshape(B * T, t.shape[-1])
    n_blocks = B * T // TM_MIX
    blk = lambda s: jnp.minimum(s, n_blocks - 1)
    row = lambda width: pl.BlockSpec((TM_MIX, width), lambda s: (blk(s), 0))
    prev = pl.BlockSpec((WINDOW, KV_W), lambda s: (jnp.maximum(blk(s) * NSUB - 1, 0), 0))
    halo = pl.BlockSpec((HALO, CONV_CH), lambda s: (jnp.maximum(blk(s) * (TM_MIX // HALO) - 1, 0), 0))
    body = functools.partial(_mix_ffn_kernel, blocks_per_seq=T // TM_MIX, n_blocks=n_blocks)
    out = pl.pallas_call(
        body,
        grid=(n_blocks + 1,),
        in_specs=[pl.BlockSpec(memory_space=pltpu.SMEM),
                  row(D_MODEL), row(Q_W), row(KV_W), prev, row(KV_W), prev, row(CONV_CH), halo,
                  row(D_MODEL), row(D_MODEL),
                  _resident((CONV_WIDTH, CONV_CH)), _resident((1, CONV_CH)), _resident((1, CONV_CH)),
                  _resident((1, CONV_CH)), _resident_layer((CONV_CH, D_MODEL), layer),
                  _resident_layer((D_MODEL, D_MODEL), layer), _resident((1, D_MODEL)),
                  _resident_layer((D_MODEL, 2 * D_FF), layer), _resident_layer((D_FF, D_MODEL), layer)],
        out_specs=pl.BlockSpec((TM_MIX, D_MODEL), lambda s: (jnp.maximum(s - 1, 0), 0)),
        out_shape=jax.ShapeDtypeStruct((B * T, D_MODEL), _F32),
        scratch_shapes=[pltpu.VMEM((HALO + TM_MIX, CONV_CH), _F32),
                        pltpu.VMEM((SUBLANES - 1, SHIFT_ROWS, CONV_CH), _F32),
                        pltpu.VMEM((TM_MIX, D_MODEL), _F32)],
        compiler_params=pltpu.CompilerParams(dimension_semantics=("arbitrary",), vmem_limit_bytes=VMEM_LIMIT),
        name="mix_ffn",
    )(sinks, flat(x), flat(q), flat(k), flat(k), flat(v), flat(v), flat(a), flat(a), flat(ga), flat(gb),
      cw, cb, lg, lb, wpw, wout, gf, wgu, wd)
    return out.reshape(B, T, D_MODEL)


def _rope_lane_tables(seq):
    half = ROT_DIM // 2
    inv_freq = ROPE_THETA ** (-jnp.arange(0, ROT_DIM, 2, dtype=_F32) / ROT_DIM)
    ang = jnp.arange(seq, dtype=_F32)[:, None] * inv_freq[None, :]
    cos, sin = jnp.cos(ang), jnp.sin(ang)
    ones = jnp.ones((seq, HEAD_DIM - ROT_DIM), _F32)
    zeros = jnp.zeros((seq, HEAD_DIM - ROT_DIM), _F32)
    zh = jnp.zeros((seq, half), _F32)
    c = jnp.concatenate([cos, cos, ones], axis=1)
    s1 = jnp.concatenate([zh, sin, zeros], axis=1)
    s2 = jnp.concatenate([-sin, zh, zeros], axis=1)
    dup = lambda t: jnp.concatenate([t, t], axis=1)
    return dup(c), dup(s1), dup(s2)


def kernel(x, norm_mix, w_in, q_norm, k_norm, sinks, conv_w, conv_b, conv_ln_g, conv_ln_b,
           w_conv_out, w_out, norm_ffn, w_gate_up, w_down):
    B, T, D = x.shape
    depth = w_in.shape[0]
    assert D == D_MODEL and T % TM_PROJ == 0 and T % TM_MIX == 0
    c, s1, s2 = _rope_lane_tables(T)
    seg = jnp.arange(MXU_DIM) // HEAD_DIM
    ones_blockdiag = (seg[:, None] == seg[None, :]).astype(_BF16)
    dup = lambda t: jnp.concatenate([t, t], axis=-1)[None, :]
    w_in, w_conv_out, w_out, w_gate_up, w_down = (
        w.astype(_BF16) for w in (w_in, w_conv_out, w_out, w_gate_up, w_down))
    for l in range(depth):
        q, k, v, a, ga, gb = _proj_call(
            l, x, norm_mix[l][None, :], w_in, dup(q_norm[l]), dup(k_norm[l]), c, s1, s2, ones_blockdiag)
        x = _mix_ffn_call(l, sinks[l], x, q, k, v, a, ga, gb, conv_w[l], conv_b[l][None, :],
                          conv_ln_g[l][None, :], conv_ln_b[l][None, :], w_conv_out, w_out,
                          norm_ffn[l][None, :], w_gate_up, w_down)
    return x
```

```python
import functools
import math

import jax
import jax.numpy as jnp
from jax import lax
from jax.experimental import pallas as pl
from jax.experimental.pallas import tpu as pltpu

D_MODEL = 1024
N_HEADS = 16
N_KV_HEADS = 2
HEAD_DIM = 64
GROUP = N_HEADS // N_KV_HEADS
ROT_DIM = HEAD_DIM // 4
ROPE_THETA = 500000.0
WINDOW = 128
CONV_CH = D_MODEL // 2
CONV_WIDTH = 31
D_FF = -(-(8 * D_MODEL) // (3 * 256)) * 256
EPS = 1e-6
Q_W = N_HEADS * HEAD_DIM
KV_W = N_KV_HEADS * HEAD_DIM
IN_W = Q_W + 2 * KV_W + 2 * CONV_CH + 2 * D_MODEL

LANES = 128
SUBLANES = 8
MXU_DIM = 256
HALO = 32
PAIRS = GROUP // 2
VMEM_LIMIT = 56 * 1024 * 1024
LOG2E = math.log2(math.e)

TM_PROJ = 512
NSUB = 2
TM_MIX = NSUB * WINDOW
FFN_CHUNK = MXU_DIM
CONV_LEAD = HALO - (CONV_WIDTH - 1)
SHIFT_ROWS = TM_MIX + HALO - SUBLANES
CONV_PITCH = CONV_CH + LANES

_F32 = jnp.float32
_BF16 = jnp.bfloat16


def _segment_sumsq(x, ones_blockdiag):
    xx = x * x
    hi = xx.astype(_BF16)
    lo = (xx - hi.astype(_F32)).astype(_BF16)
    width = ones_blockdiag.shape[0]
    parts = []
    for c in range(x.shape[1] // width):
        sl = slice(c * width, (c + 1) * width)
        parts.append(jnp.dot(hi[:, sl], ones_blockdiag, preferred_element_type=_F32)
                     + jnp.dot(lo[:, sl], ones_blockdiag, preferred_element_type=_F32))
    return parts[0] if len(parts) == 1 else jnp.concatenate(parts, axis=1)


def _norm_rope(xg, ss, w, c, s1, s2):
    xn = xg * lax.rsqrt(ss * (1.0 / HEAD_DIM) + EPS) * w
    return xn * c + pltpu.roll(xn, ROT_DIM // 2, 1) * s1 + pltpu.roll(xn, LANES - ROT_DIM // 2, 1) * s2


def _proj_kernel(x_ref, g_ref, w_ref, qw_ref, kw_ref, c_ref, s1_ref, s2_ref, ones_ref,
                 q_out, k_out, v_out, a_out, ga_out, gb_out):
    x = x_ref[0]
    ms = jnp.mean(x * x, axis=-1, keepdims=True)
    h = (x * lax.rsqrt(ms + EPS) * g_ref[...]).astype(_BF16)
    c, s1, s2 = c_ref[...], s1_ref[...], s2_ref[...]

    q = jnp.dot(h, w_ref[:, :Q_W], preferred_element_type=_F32)
    q_ss = _segment_sumsq(q, ones_ref[...])
    qw = qw_ref[...]
    scale = HEAD_DIM ** -0.5 * LOG2E
    for g in range(Q_W // LANES):
        sl = slice(g * LANES, (g + 1) * LANES)
        q_out[0, :, sl] = (_norm_rope(q[:, sl], q_ss[:, sl], qw, c, s1, s2) * scale).astype(_BF16)

    kv = jnp.dot(h, w_ref[:, Q_W:Q_W + 2 * KV_W], preferred_element_type=_F32)
    k = kv[:, :KV_W]
    k_ss = _segment_sumsq(k, ones_ref[:KV_W, :KV_W])
    k_out[0] = _norm_rope(k, k_ss, kw_ref[...], c, s1, s2).astype(_BF16)
    v_out[0] = kv[:, KV_W:].astype(_BF16)

    o = Q_W + 2 * KV_W
    uu = jnp.dot(h, w_ref[:, o:o + 2 * CONV_CH], preferred_element_type=_F32)
    a_out[0] = uu[:, :CONV_CH] * jax.nn.sigmoid(uu[:, CONV_CH:])

    o += 2 * CONV_CH
    gates = jnp.dot(h, w_ref[:, o:o + 2 * D_MODEL], preferred_element_type=_F32)
    ga_out[0] = jax.nn.sigmoid(gates[:, :D_MODEL]).astype(_BF16)
    gb_out[0] = jax.nn.sigmoid(gates[:, D_MODEL:]).astype(_BF16)


def _attend_prep(kband, vband):
    lane = lax.broadcasted_iota(jnp.int32, kband.shape, 1)
    low = lane < HEAD_DIM
    zero = jnp.zeros_like(kband)
    rows = 2 * PAIRS * WINDOW
    qi = lax.broadcasted_iota(jnp.int32, (rows, WINDOW), 0) & (WINDOW - 1)
    kj = lax.broadcasted_iota(jnp.int32, (rows, WINDOW), 1)
    return dict(
        k=(kband, pltpu.roll(kband, HEAD_DIM, 1)), v=(vband, pltpu.roll(vband, HEAD_DIM, 1)),
        low=low, zero=zero,
        ones_lo=low.astype(_F32).astype(_BF16), ones_hi=(1.0 - low.astype(_F32)).astype(_BF16),
        tri=kj <= qi,
        low_o=lax.broadcasted_iota(jnp.int32, (PAIRS * WINDOW, LANES), 1) < HEAD_DIM)


def _attend_scores(q, prep, g):
    low, zero = prep["low"], prep["zero"]
    k_even = jnp.where(low, prep["k"][g], zero)
    k_odd = jnp.where(low, zero, prep["k"][1 - g])
    base = g * GROUP * HEAD_DIM
    qp = jnp.concatenate([q[:, base + t * LANES: base + (t + 1) * LANES] for t in range(PAIRS)], axis=0)
    contract_last = (((1,), (1,)), ((), ()))
    return jnp.concatenate([lax.dot_general(qp, k_even, contract_last, preferred_element_type=_F32),
                            lax.dot_general(qp, k_odd, contract_last, preferred_element_type=_F32)], axis=0)


def _attend_softmax(s, prep, g, sink_of, first_block):
    tri = prep["tri"]
    sc = jnp.where(tri, s[:, :WINDOW], s[:, WINDOW:])
    if first_block is not None:
        sc = jnp.where(jnp.logical_or(tri, jnp.logical_not(first_block)), sc, -jnp.inf)
    sink = jnp.concatenate(
        [jnp.full((WINDOW, 1), sink_of(g * GROUP + 2 * t + par), _F32)
         for par in range(2) for t in range(PAIRS)], axis=0)
    m = jnp.maximum(jnp.max(sc, axis=-1, keepdims=True), sink)
    p = jnp.exp2(sc - m)
    pz = jnp.zeros_like(p)
    pb = jnp.concatenate([jnp.where(tri, p, pz), jnp.where(tri, pz, p)], axis=1).astype(_BF16)
    return pb, jnp.exp2(sink - m)


def _attend_values(pb, sink_p, prep, g):
    low, zero = prep["low"], prep["zero"]
    half = PAIRS * WINDOW
    w_even = jnp.concatenate([jnp.where(low, prep["v"][g], zero), prep["ones_lo"]], axis=1)
    w_odd = jnp.concatenate([jnp.where(low, zero, prep["v"][1 - g]), prep["ones_hi"]], axis=1)
    wide = (jnp.dot(pb[:half], w_even, preferred_element_type=_F32)
            + jnp.dot(pb[half:], w_odd, preferred_element_type=_F32))
    den = wide[:, LANES:] + jnp.where(prep["low_o"], sink_p[:half], sink_p[half:])
    o = wide[:, :LANES] / den
    return [o[t * WINDOW:(t + 1) * WINDOW] for t in range(PAIRS)]


def _mix_ffn_kernel(sinks_ref, x_ref, q_ref, kc_ref, kp_ref, vc_ref, vp_ref, ac_ref, ah_ref,
                    ga_ref, gb_ref, cw_ref, cb_ref, lg_ref, lb_ref, wpw_ref, wout_ref,
                    gf_ref, wgu_ref, wd_ref,
                    o_ref, aext_ref, shift_ref, x1_ref, *, blocks_per_seq, n_blocks):
    s = pl.program_id(0)
    first = jnp.minimum(s, n_blocks - 1) % blocks_per_seq == 0

    @pl.when(s == 0)
    def _():
        x1_ref[...] = jnp.zeros_like(x1_ref)

    x1 = x1_ref[...]
    ms = jnp.mean(x1 * x1, axis=-1, keepdims=True)
    h = (x1 * lax.rsqrt(ms + EPS) * gf_ref[...]).astype(_BF16)
    ffn = {"acc": x1, "next": 0}

    def ffn_chunks(n, after=None):
        for _ in range(n):
            c = ffn["next"]
            if c >= D_FF // FFN_CHUNK:
                return
            hc = h
            if after is not None:
                parts = [after[i:i + SUBLANES] for i in range(0, after.shape[0], SUBLANES)]
                while len(parts) > 1:
                    parts = [a + b for a, b in zip(parts[::2], parts[1::2])] + parts[len(parts) & ~1:]
                u = pltpu.bitcast(parts[0], jnp.uint32)
                z = lax.shift_right_logical(lax.shift_right_logical(u, jnp.uint32(16)), jnp.uint32(16))
                ninf = pltpu.bitcast(z | jnp.uint32(0xFF800000), _F32)
                ninf = jnp.tile(ninf, (2, 1)).astype(_BF16)
                hc = jnp.maximum(h, jnp.tile(ninf, (TM_MIX // (2 * SUBLANES), D_MODEL // LANES)))
            cols = slice(c * FFN_CHUNK, (c + 1) * FFN_CHUNK)
            gate = jnp.dot(hc, wgu_ref[:, cols], preferred_element_type=_F32)
            up = jnp.dot(hc, wgu_ref[:, D_FF + c * FFN_CHUNK:D_FF + (c + 1) * FFN_CHUNK],
                         preferred_element_type=_F32)
            act = (gate * jax.nn.sigmoid(gate) * up).astype(_BF16)
            ffn["acc"] = ffn["acc"] + jnp.dot(act, wd_ref[cols, :], preferred_element_type=_F32)
            ffn["next"] = c + 1

    aext_ref[:HALO, :CONV_CH] = jnp.where(first, jnp.zeros_like(ah_ref[...]), ah_ref[...])
    aext_ref[HALO:, :CONV_CH] = ac_ref[...]
    for r in range(1, SUBLANES):
        shift_ref[r - 1, :, :CONV_CH] = aext_ref[r:r + SHIFT_ROWS, :CONV_CH]
    ffn_chunks(1)

    y_groups = []
    for grp in range(CONV_CH // LANES):
        lanes = slice(grp * LANES, (grp + 1) * LANES)
        y = jnp.broadcast_to(cb_ref[:, lanes], (TM_MIX, LANES))
        for j in range(CONV_WIDTH):
            off = CONV_LEAD + j
            r, base = off % SUBLANES, off - off % SUBLANES
            win = (aext_ref[base:base + TM_MIX, lanes] if r == 0
                   else shift_ref[r - 1, base:base + TM_MIX, lanes])
            y = y + cw_ref[j:j + 1, lanes] * win
        y_groups.append(y)
        ffn_chunks(1, after=y)
    y = jnp.concatenate(y_groups, axis=1)
    mu = jnp.mean(y, axis=-1, keepdims=True)
    yc = y - mu
    var = jnp.mean(yc * yc, axis=-1, keepdims=True)
    yn = yc * lax.rsqrt(var + EPS) * lg_ref[...] + lb_ref[...]
    cact = (yn * jax.nn.sigmoid(yn)).astype(_BF16)
    c_out = jnp.dot(cact, wpw_ref[...], preferred_element_type=_F32)
    ffn_chunks(1)

    k_all = jnp.concatenate([kp_ref[...], kc_ref[...]], axis=0)
    v_all = jnp.concatenate([vp_ref[...], vc_ref[...]], axis=0)
    sink_of = lambda head: sinks_ref[head] * LOG2E
    blocks = []
    for j in range(NSUB):
        cur = slice((j + 1) * WINDOW, (j + 2) * WINDOW)
        prv = slice(j * WINDOW, (j + 1) * WINDOW)
        prep = _attend_prep(jnp.concatenate([k_all[cur], k_all[prv]], axis=0),
                            jnp.concatenate([v_all[cur], v_all[prv]], axis=0))
        q = q_ref[j * WINDOW:(j + 1) * WINDOW, :]
        slabs = []
        for g in range(N_KV_HEADS):
            sc = _attend_scores(q, prep, g)
            pb, sink_p = _attend_softmax(sc, prep, g, sink_of, first if j == 0 else None)
            ffn_chunks(1)
            slabs.extend(_attend_values(pb, sink_p, prep, g))
        blocks.append(jnp.concatenate(slabs, axis=1))
    a_att = jnp.concatenate(blocks, axis=0)

    merged = ga_ref[...].astype(_F32) * a_att + gb_ref[...].astype(_F32) * c_out
    x1_new = x_ref[...] + jnp.dot(merged.astype(_BF16), wout_ref[...], preferred_element_type=_F32)
    ffn_chunks(D_FF // FFN_CHUNK)
    o_ref[...] = ffn["acc"]
    x1_ref[...] = x1_new


def _resident(shape):
    return pl.BlockSpec(shape, lambda *_: (0,) * len(shape), pipeline_mode=pl.Buffered(1))


def _resident_layer(shape, layer):
    return pl.BlockSpec((None,) + shape, lambda *_: (layer,) + (0,) * len(shape), pipeline_mode=pl.Buffered(1))


def _params():
    return pltpu.CompilerParams(dimension_semantics=("parallel", "parallel"), vmem_limit_bytes=VMEM_LIMIT)


def _proj_call(layer, x, g, w, qw, kw, c, s1, s2, ones_blockdiag):
    B, T, _ = x.shape
    row = lambda width: pl.BlockSpec((1, TM_PROJ, width), lambda b, i: (b, i, 0))
    tab = pl.BlockSpec((TM_PROJ, LANES), lambda b, i: (i, 0))
    out = lambda width, dt: jax.ShapeDtypeStruct((B, T, width), dt)
    return pl.pallas_call(
        _proj_kernel,
        grid=(B, T // TM_PROJ),
        in_specs=[row(D_MODEL), _resident((1, D_MODEL)), _resident_layer((D_MODEL, IN_W), layer),
                  _resident((1, LANES)), _resident((1, LANES)), tab, tab, tab,
                  _resident((MXU_DIM, MXU_DIM))],
        out_specs=[row(Q_W), row(KV_W), row(KV_W), row(CONV_CH), row(D_MODEL), row(D_MODEL)],
        out_shape=[out(Q_W, _BF16), out(KV_W, _BF16), out(KV_W, _BF16), out(CONV_CH, _F32),
                   out(D_MODEL, _BF16), out(D_MODEL, _BF16)],
        compiler_params=_params(),
        name="proj",
    )(x, g, w, qw, kw, c, s1, s2, ones_blockdiag)


def _mix_ffn_call(layer, sinks, x, q, k, v, a, ga, gb, cw, cb, lg, lb, wpw, wout, gf, wgu, wd):
    B, T, _ = x.shape
    flat = lambda t: t.reshape(B * T, t.shape[-1])
    n_blocks = B * T // TM_MIX
    blk = lambda s: jnp.minimum(s, n_blocks - 1)
    row = lambda width: pl.BlockSpec((TM_MIX, width), lambda s: (blk(s), 0))
    prev = pl.BlockSpec((WINDOW, KV_W), lambda s: (jnp.maximum(blk(s) * NSUB - 1, 0), 0))
    halo = pl.BlockSpec((HALO, CONV_CH), lambda s: (jnp.maximum(blk(s) * (TM_MIX // HALO) - 1, 0), 0))
    body = functools.partial(_mix_ffn_kernel, blocks_per_seq=T // TM_MIX, n_blocks=n_blocks)
    out = pl.pallas_call(
        body,
        grid=(n_blocks + 1,),
        in_specs=[pl.BlockSpec(memory_space=pltpu.SMEM),
                  row(D_MODEL), row(Q_W), row(KV_W), prev, row(KV_W), prev, row(CONV_CH), halo,
                  row(D_MODEL), row(D_MODEL),
                  _resident((CONV_WIDTH, CONV_CH)), _resident((1, CONV_CH)), _resident((1, CONV_CH)),
                  _resident((1, CONV_CH)), _resident_layer((CONV_CH, D_MODEL), layer),
                  _resident_layer((D_MODEL, D_MODEL), layer), _resident((1, D_MODEL)),
                  _resident_layer((D_MODEL, 2 * D_FF), layer), _resident_layer((D_FF, D_MODEL), layer)],
        out_specs=pl.BlockSpec((TM_MIX, D_MODEL), lambda s: (jnp.maximum(s - 1, 0), 0)),
        out_shape=jax.ShapeDtypeStruct((B * T, D_MODEL), _F32),
        scratch_shapes=[pltpu.VMEM((HALO + TM_MIX, CONV_PITCH), _F32),
                        pltpu.VMEM((SUBLANES - 1, SHIFT_ROWS, CONV_PITCH), _F32),
                        pltpu.VMEM((TM_MIX, D_MODEL), _F32)],
        compiler_params=pltpu.CompilerParams(dimension_semantics=("arbitrary",), vmem_limit_bytes=VMEM_LIMIT),
        name="mix_ffn",
    )(sinks, flat(x), flat(q), flat(k), flat(k), flat(v), flat(v), flat(a), flat(a), flat(ga), flat(gb),
      cw, cb, lg, lb, wpw, wout, gf, wgu, wd)
    return out.reshape(B, T, D_MODEL)


def _rope_lane_tables(seq):
    half = ROT_DIM // 2
    inv_freq = ROPE_THETA ** (-jnp.arange(0, ROT_DIM, 2, dtype=_F32) / ROT_DIM)
    ang = jnp.arange(seq, dtype=_F32)[:, None] * inv_freq[None, :]
    cos, sin = jnp.cos(ang), jnp.sin(ang)
    ones = jnp.ones((seq, HEAD_DIM - ROT_DIM), _F32)
    zeros = jnp.zeros((seq, HEAD_DIM - ROT_DIM), _F32)
    zh = jnp.zeros((seq, half), _F32)
    c = jnp.concatenate([cos, cos, ones], axis=1)
    s1 = jnp.concatenate([zh, sin, zeros], axis=1)
    s2 = jnp.concatenate([-sin, zh, zeros], axis=1)
    dup = lambda t: jnp.concatenate([t, t], axis=1)
    return dup(c), dup(s1), dup(s2)


def kernel(x, norm_mix, w_in, q_norm, k_norm, sinks, conv_w, conv_b, conv_ln_g, conv_ln_b,
           w_conv_out, w_out, norm_ffn, w_gate_up, w_down):
    B, T, D = x.shape
    depth = w_in.shape[0]
    assert D == D_MODEL and T % TM_PROJ == 0 and T % TM_MIX == 0
    c, s1, s2 = _rope_lane_tables(T)
    seg = jnp.arange(MXU_DIM) // HEAD_DIM
    ones_blockdiag = (seg[:, None] == seg[None, :]).astype(_BF16)
    dup = lambda t: jnp.concatenate([t, t], axis=-1)[None, :]
    w_in, w_conv_out, w_out, w_gate_up, w_down = (
        w.astype(_BF16) for w in (w_in, w_conv_out, w_out, w_gate_up, w_down))
    for l in range(depth):
        q, k, v, a, ga, gb = _proj_call(
            l, x, norm_mix[l][None, :], w_in, dup(q_norm[l]), dup(k_norm[l]), c, s1, s2, ones_blockdiag)
        x = _mix_ffn_call(l, sinks[l], x, q, k, v, a, ga, gb, conv_w[l], conv_b[l][None, :],
                          conv_ln_g[l][None, :], conv_ln_b[l][None, :], w_conv_out, w_out,
                          norm_ffn[l][None, :], w_gate_up, w_down)
    return x
```

```python
import functools
import math

import jax
import jax.numpy as jnp
from jax import lax
from jax.experimental import pallas as pl
from jax.experimental.pallas import tpu as pltpu

D_MODEL = 1024
N_HEADS = 16
N_KV_HEADS = 2
HEAD_DIM = 64
GROUP = N_HEADS // N_KV_HEADS
ROT_DIM = HEAD_DIM // 4
ROPE_THETA = 500000.0
WINDOW = 128
CONV_CH = D_MODEL // 2
CONV_WIDTH = 31
D_FF = -(-(8 * D_MODEL) // (3 * 256)) * 256
EPS = 1e-6
Q_W = N_HEADS * HEAD_DIM
KV_W = N_KV_HEADS * HEAD_DIM
IN_W = Q_W + 2 * KV_W + 2 * CONV_CH + 2 * D_MODEL

LANES = 128
SUBLANES = 8
MXU_DIM = 256
HALO = 32
PAIRS = GROUP // 2
VMEM_LIMIT = 56 * 1024 * 1024
LOG2E = math.log2(math.e)

TM_PROJ = 1024
NSUB = 4
TM_MIX = NSUB * WINDOW
CONV_LEAD = HALO - (CONV_WIDTH - 1)
SHIFT_ROWS = TM_MIX + HALO - SUBLANES

_F32 = jnp.float32
_BF16 = jnp.bfloat16


def _segment_sumsq(x, ones_blockdiag):
    xx = x * x
    hi = xx.astype(_BF16)
    lo = (xx - hi.astype(_F32)).astype(_BF16)
    width = ones_blockdiag.shape[0]
    parts = []
    for c in range(x.shape[1] // width):
        sl = slice(c * width, (c + 1) * width)
        parts.append(jnp.dot(hi[:, sl], ones_blockdiag, preferred_element_type=_F32)
                     + jnp.dot(lo[:, sl], ones_blockdiag, preferred_element_type=_F32))
    return parts[0] if len(parts) == 1 else jnp.concatenate(parts, axis=1)


def _norm_rope(xg, ss, w, c, s1, s2):
    xn = xg * lax.rsqrt(ss * (1.0 / HEAD_DIM) + EPS) * w
    return xn * c + pltpu.roll(xn, ROT_DIM // 2, 1) * s1 + pltpu.roll(xn, LANES - ROT_DIM // 2, 1) * s2


def _proj_kernel(x_ref, g_ref, w_ref, qw_ref, kw_ref, c_ref, s1_ref, s2_ref, ones_ref,
                 q_out, k_out, v_out, a_out, ga_out, gb_out):
    x = x_ref[0]
    ms = jnp.mean(x * x, axis=-1, keepdims=True)
    h = (x * lax.rsqrt(ms + EPS) * g_ref[...]).astype(_BF16)
    c, s1, s2 = c_ref[...], s1_ref[...], s2_ref[...]

    q = jnp.dot(h, w_ref[:, :Q_W], preferred_element_type=_F32)
    q_ss = _segment_sumsq(q, ones_ref[...])
    qw = qw_ref[...]
    scale = HEAD_DIM ** -0.5 * LOG2E
    for g in range(Q_W // LANES):
        sl = slice(g * LANES, (g + 1) * LANES)
        q_out[0, :, sl] = (_norm_rope(q[:, sl], q_ss[:, sl], qw, c, s1, s2) * scale).astype(_BF16)

    kv = jnp.dot(h, w_ref[:, Q_W:Q_W + 2 * KV_W], preferred_element_type=_F32)
    k = kv[:, :KV_W]
    k_ss = _segment_sumsq(k, ones_ref[:KV_W, :KV_W])
    k_out[0] = _norm_rope(k, k_ss, kw_ref[...], c, s1, s2).astype(_BF16)
    v_out[0] = kv[:, KV_W:].astype(_BF16)

    o = Q_W + 2 * KV_W
    uu = jnp.dot(h, w_ref[:, o:o + 2 * CONV_CH], preferred_element_type=_F32)
    a_out[0] = uu[:, :CONV_CH] * jax.nn.sigmoid(uu[:, CONV_CH:])

    o += 2 * CONV_CH
    gates = jnp.dot(h, w_ref[:, o:o + 2 * D_MODEL], preferred_element_type=_F32)
    ga_out[0] = jax.nn.sigmoid(gates[:, :D_MODEL]).astype(_BF16)
    gb_out[0] = jax.nn.sigmoid(gates[:, D_MODEL:]).astype(_BF16)


def _attend_prep(kband, vband):
    lane = lax.broadcasted_iota(jnp.int32, kband.shape, 1)
    low = lane < HEAD_DIM
    zero = jnp.zeros_like(kband)
    rows = 2 * PAIRS * WINDOW
    qi = lax.broadcasted_iota(jnp.int32, (rows, WINDOW), 0) & (WINDOW - 1)
    kj = lax.broadcasted_iota(jnp.int32, (rows, WINDOW), 1)
    return dict(
        k=(kband, pltpu.roll(kband, HEAD_DIM, 1)), v=(vband, pltpu.roll(vband, HEAD_DIM, 1)),
        low=low, zero=zero,
        ones_lo=low.astype(_F32).astype(_BF16), ones_hi=(1.0 - low.astype(_F32)).astype(_BF16),
        tri=kj <= qi,
        low_o=lax.broadcasted_iota(jnp.int32, (PAIRS * WINDOW, LANES), 1) < HEAD_DIM)


def _attend_scores(q, prep, g):
    low, zero = prep["low"], prep["zero"]
    k_even = jnp.where(low, prep["k"][g], zero)
    k_odd = jnp.where(low, zero, prep["k"][1 - g])
    base = g * GROUP * HEAD_DIM
    qp = jnp.concatenate([q[:, base + t * LANES: base + (t + 1) * LANES] for t in range(PAIRS)], axis=0)
    contract_last = (((1,), (1,)), ((), ()))
    return jnp.concatenate([lax.dot_general(qp, k_even, contract_last, preferred_element_type=_F32),
                            lax.dot_general(qp, k_odd, contract_last, preferred_element_type=_F32)], axis=0)


def _attend_softmax(s, prep, g, sink_of, first_block):
    tri = prep["tri"]
    sc = jnp.where(tri, s[:, :WINDOW], s[:, WINDOW:])
    if first_block is not None:
        sc = jnp.where(jnp.logical_or(tri, jnp.logical_not(first_block)), sc, -jnp.inf)
    sink = jnp.concatenate(
        [jnp.full((WINDOW, 1), sink_of(g * GROUP + 2 * t + par), _F32)
         for par in range(2) for t in range(PAIRS)], axis=0)
    m = jnp.maximum(jnp.max(sc, axis=-1, keepdims=True), sink)
    p = jnp.exp2(sc - m)
    pz = jnp.zeros_like(p)
    pb = jnp.concatenate([jnp.where(tri, p, pz), jnp.where(tri, pz, p)], axis=1).astype(_BF16)
    return pb, jnp.exp2(sink - m)


def _attend_values(pb, sink_p, prep, g):
    low, zero = prep["low"], prep["zero"]
    half = PAIRS * WINDOW
    w_even = jnp.concatenate([jnp.where(low, prep["v"][g], zero), prep["ones_lo"]], axis=1)
    w_odd = jnp.concatenate([jnp.where(low, zero, prep["v"][1 - g]), prep["ones_hi"]], axis=1)
    wide = (jnp.dot(pb[:half], w_even, preferred_element_type=_F32)
            + jnp.dot(pb[half:], w_odd, preferred_element_type=_F32))
    den = wide[:, LANES:] + jnp.where(prep["low_o"], sink_p[:half], sink_p[half:])
    o = wide[:, :LANES] / den
    return [o[t * WINDOW:(t + 1) * WINDOW] for t in range(PAIRS)]


def _mix_ffn_kernel(sinks_ref, x_ref, q_ref, kc_ref, kp_ref, vc_ref, vp_ref, ac_ref, ah_ref,
                    ga_ref, gb_ref, cw_ref, cb_ref, lg_ref, lb_ref, wpw_ref, wout_ref,
                    gf_ref, wgu_ref, wd_ref,
                    o_ref, aext_ref, shift_ref, x1_ref, *, blocks_per_seq, n_blocks):
    s = pl.program_id(0)
    first = jnp.minimum(s, n_blocks - 1) % blocks_per_seq == 0

    @pl.when(s == 0)
    def _():
        x1_ref[...] = jnp.zeros_like(x1_ref)

    k_all = jnp.concatenate([kp_ref[...], kc_ref[...]], axis=0)
    v_all = jnp.concatenate([vp_ref[...], vc_ref[...]], axis=0)
    sink_of = lambda head: sinks_ref[head] * LOG2E
    blocks = []
    for j in range(NSUB):
        cur = slice((j + 1) * WINDOW, (j + 2) * WINDOW)
        prv = slice(j * WINDOW, (j + 1) * WINDOW)
        prep = _attend_prep(jnp.concatenate([k_all[cur], k_all[prv]], axis=0),
                            jnp.concatenate([v_all[cur], v_all[prv]], axis=0))
        q = q_ref[j * WINDOW:(j + 1) * WINDOW, :]
        slabs = []
        for g in range(N_KV_HEADS):
            sc = _attend_scores(q, prep, g)
            pb, sink_p = _attend_softmax(sc, prep, g, sink_of, first if j == 0 else None)
            slabs.extend(_attend_values(pb, sink_p, prep, g))
        blocks.append(jnp.concatenate(slabs, axis=1))
    a_att = jnp.concatenate(blocks, axis=0)

    aext_ref[:HALO] = jnp.where(first, jnp.zeros_like(ah_ref[...]), ah_ref[...])
    aext_ref[HALO:] = ac_ref[...]
    for r in range(1, SUBLANES):
        shift_ref[r - 1] = aext_ref[r:r + SHIFT_ROWS, :]
    y = jnp.broadcast_to(cb_ref[...], (TM_MIX, CONV_CH))
    for j in range(CONV_WIDTH):
        off = CONV_LEAD + j
        r, base = off % SUBLANES, off - off % SUBLANES
        win = aext_ref[base:base + TM_MIX, :] if r == 0 else shift_ref[r - 1, base:base + TM_MIX, :]
        y = y + cw_ref[j:j + 1, :] * win
    mu = jnp.mean(y, axis=-1, keepdims=True)
    yc = y - mu
    var = jnp.mean(yc * yc, axis=-1, keepdims=True)
    yn = yc * lax.rsqrt(var + EPS) * lg_ref[...] + lb_ref[...]
    cact = (yn * jax.nn.sigmoid(yn)).astype(_BF16)
    c_out = jnp.dot(cact, wpw_ref[...], preferred_element_type=_F32)

    merged = ga_ref[...].astype(_F32) * a_att + gb_ref[...].astype(_F32) * c_out
    x1_new = x_ref[...] + jnp.dot(merged.astype(_BF16), wout_ref[...], preferred_element_type=_F32)

    x1 = x1_ref[...]
    ms = jnp.mean(x1 * x1, axis=-1, keepdims=True)
    h = (x1 * lax.rsqrt(ms + EPS) * gf_ref[...]).astype(_BF16)
    gate = jnp.dot(h, wgu_ref[:, :D_FF], preferred_element_type=_F32)
    up = jnp.dot(h, wgu_ref[:, D_FF:], preferred_element_type=_F32)
    act = (gate * jax.nn.sigmoid(gate) * up).astype(_BF16)
    o_ref[...] = x1_ref[...] + jnp.dot(act, wd_ref[...], preferred_element_type=_F32)

    x1_ref[...] = x1_new


def _resident(shape):
    return pl.BlockSpec(shape, lambda *_: (0,) * len(shape), pipeline_mode=pl.Buffered(1))


def _resident_layer(shape, layer):
    return pl.BlockSpec((None,) + shape, lambda *_: (layer,) + (0,) * len(shape), pipeline_mode=pl.Buffered(1))


def _params():
    return pltpu.CompilerParams(dimension_semantics=("parallel", "parallel"), vmem_limit_bytes=VMEM_LIMIT)


def _proj_call(layer, x, g, w, qw, kw, c, s1, s2, ones_blockdiag):
    B, T, _ = x.shape
    row = lambda width: pl.BlockSpec((1, TM_PROJ, width), lambda b, i: (b, i, 0))
    tab = pl.BlockSpec((TM_PROJ, LANES), lambda b, i: (i, 0))
    out = lambda width, dt: jax.ShapeDtypeStruct((B, T, width), dt)
    return pl.pallas_call(
        _proj_kernel,
        grid=(B, T // TM_PROJ),
        in_specs=[row(D_MODEL), _resident((1, D_MODEL)), _resident_layer((D_MODEL, IN_W), layer),
                  _resident((1, LANES)), _resident((1, LANES)), tab, tab, tab,
                  _resident((MXU_DIM, MXU_DIM))],
        out_specs=[row(Q_W), row(KV_W), row(KV_W), row(CONV_CH), row(D_MODEL), row(D_MODEL)],
        out_shape=[out(Q_W, _BF16), out(KV_W, _BF16), out(KV_W, _BF16), out(CONV_CH, _F32),
                   out(D_MODEL, _BF16), out(D_MODEL, _BF16)],
        compiler_params=_params(),
        name="proj",
    )(x, g, w, qw, kw, c, s1, s2, ones_blockdiag)


def _mix_ffn_call(layer, sinks, x, q, k, v, a, ga, gb, cw, cb, lg, lb, wpw, wout, gf, wgu, wd):
    B, T, _ = x.shape
    flat = lambda t: t.reshape(B * T, t.shape[-1])
    n_blocks = B * T // TM_MIX
    blk = lambda s: jnp.minimum(s, n_blocks - 1)
    row = lambda width: pl.BlockSpec((TM_MIX, width), lambda s: (blk(s), 0))
    prev = pl.BlockSpec((WINDOW, KV_W), lambda s: (jnp.maximum(blk(s) * NSUB - 1, 0), 0))
    halo = pl.BlockSpec((HALO, CONV_CH), lambda s: (jnp.maximum(blk(s) * (TM_MIX // HALO) - 1, 0), 0))
    body = functools.partial(_mix_ffn_kernel, blocks_per_seq=T // TM_MIX, n_blocks=n_blocks)
    out = pl.pallas_call(
        body,
        grid=(n_blocks + 1,),
        in_specs=[pl.BlockSpec(memory_space=pltpu.SMEM),
                  row(D_MODEL), row(Q_W), row(KV_W), prev, row(KV_W), prev, row(CONV_CH), halo,
                  row(D_MODEL), row(D_MODEL),
                  _resident((CONV_WIDTH, CONV_CH)), _resident((1, CONV_CH)), _resident((1, CONV_CH)),
                  _resident((1, CONV_CH)), _resident_layer((CONV_CH, D_MODEL), layer),
                  _resident_layer((D_MODEL, D_MODEL), layer), _resident((1, D_MODEL)),
                  _resident_layer((D_MODEL, 2 * D_FF), layer), _resident_layer((D_FF, D_MODEL), layer)],
        out_specs=pl.BlockSpec((TM_MIX, D_MODEL), lambda s: (jnp.maximum(s - 1, 0), 0)),
        out_shape=jax.ShapeDtypeStruct((B * T, D_MODEL), _F32),
        scratch_shapes=[pltpu.VMEM((HALO + TM_MIX, CONV_CH), _F32),
                        pltpu.VMEM((SUBLANES - 1, SHIFT_ROWS, CONV_CH), _F32),
                        pltpu.VMEM((TM_MIX, D_MODEL), _F32)],
        compiler_params=pltpu.CompilerParams(dimension_semantics=("arbitrary",), vmem_limit_bytes=VMEM_LIMIT),
        name="mix_ffn",
    )(sinks, flat(x), flat(q), flat(k), flat(k), flat(v), flat(v), flat(a), flat(a), flat(ga), flat(gb),
      cw, cb, lg, lb, wpw, wout, gf, wgu, wd)
    return out.reshape(B, T, D_MODEL)


def _rope_lane_tables(seq):
    half = ROT_DIM // 2
    inv_freq = ROPE_THETA ** (-jnp.arange(0, ROT_DIM, 2, dtype=_F32) / ROT_DIM)
    dim = jnp.arange(LANES) % HEAD_DIM
    ang = jnp.arange(seq, dtype=_F32)[:, None] * inv_freq[dim % half][None, :]
    cos, sin = jnp.cos(ang), jnp.sin(ang)
    c = jnp.where(dim < ROT_DIM, cos, 1.0)
    s1 = jnp.where((dim >= half) & (dim < ROT_DIM), sin, 0.0)
    s2 = jnp.where(dim < half, -sin, 0.0)
    return c, s1, s2


def kernel(x, norm_mix, w_in, q_norm, k_norm, sinks, conv_w, conv_b, conv_ln_g, conv_ln_b,
           w_conv_out, w_out, norm_ffn, w_gate_up, w_down):
    B, T, D = x.shape
    depth = w_in.shape[0]
    assert D == D_MODEL and T % TM_PROJ == 0 and T % TM_MIX == 0
    c, s1, s2 = _rope_lane_tables(T)
    seg = jnp.arange(MXU_DIM) // HEAD_DIM
    ones_blockdiag = (seg[:, None] == seg[None, :]).astype(_BF16)
    dup = lambda t: jnp.concatenate([t, t], axis=-1)[None, :]
    w_in, w_conv_out, w_out, w_gate_up, w_down = (
        w.astype(_BF16) for w in (w_in, w_conv_out, w_out, w_gate_up, w_down))
    for l in range(depth):
        q, k, v, a, ga, gb = _proj_call(
            l, x, norm_mix[l][None, :], w_in, dup(q_norm[l]), dup(k_norm[l]), c, s1, s2, ones_blockdiag)
        x = _mix_ffn_call(l, sinks[l], x, q, k, v, a, ga, gb, conv_w[l], conv_b[l][None, :],
                          conv_ln_g[l][None, :], conv_ln_b[l][None, :], w_conv_out, w_out,
                          norm_ffn[l][None, :], w_gate_up, w_down)
    return x
```

```python
import functools
import math

import jax
import jax.numpy as jnp
from jax import lax
from jax.experimental import pallas as pl
from jax.experimental.pallas import tpu as pltpu

D_MODEL = 1024
N_HEADS = 16
N_KV_HEADS = 2
HEAD_DIM = 64
GROUP = N_HEADS // N_KV_HEADS
ROT_DIM = HEAD_DIM // 4
ROPE_THETA = 500000.0
WINDOW = 128
CONV_CH = D_MODEL // 2
CONV_WIDTH = 31
D_FF = -(-(8 * D_MODEL) // (3 * 256)) * 256
EPS = 1e-6
Q_W = N_HEADS * HEAD_DIM
KV_W = N_KV_HEADS * HEAD_DIM
IN_W = Q_W + 2 * KV_W + 2 * CONV_CH + 2 * D_MODEL

LANES = 128
SUBLANES = 8
MXU_DIM = 256
HALO = 32
PAIRS = GROUP // 2
VMEM_LIMIT = 56 * 1024 * 1024
LOG2E = math.log2(math.e)

TM_PROJ = 1024
NSUB = 2
TM_MIX = NSUB * WINDOW
CONV_LEAD = HALO - (CONV_WIDTH - 1)
SHIFT_ROWS = TM_MIX + HALO - SUBLANES

_F32 = jnp.float32
_BF16 = jnp.bfloat16


def _segment_sumsq(x, ones_blockdiag):
    xx = x * x
    hi = xx.astype(_BF16)
    lo = (xx - hi.astype(_F32)).astype(_BF16)
    width = ones_blockdiag.shape[0]
    parts = []
    for c in range(x.shape[1] // width):
        sl = slice(c * width, (c + 1) * width)
        parts.append(jnp.dot(hi[:, sl], ones_blockdiag, preferred_element_type=_F32)
                     + jnp.dot(lo[:, sl], ones_blockdiag, preferred_element_type=_F32))
    return parts[0] if len(parts) == 1 else jnp.concatenate(parts, axis=1)


def _norm_rope(xg, ss, w, c, s1, s2):
    xn = xg * lax.rsqrt(ss * (1.0 / HEAD_DIM) + EPS) * w
    return xn * c + pltpu.roll(xn, ROT_DIM // 2, 1) * s1 + pltpu.roll(xn, LANES - ROT_DIM // 2, 1) * s2


def _proj_kernel(x_ref, g_ref, w_ref, qw_ref, kw_ref, c_ref, s1_ref, s2_ref, ones_ref,
                 q_out, k_out, v_out, a_out, ga_out, gb_out):
    x = x_ref[0]
    ms = jnp.mean(x * x, axis=-1, keepdims=True)
    h = (x * lax.rsqrt(ms + EPS) * g_ref[...]).astype(_BF16)
    c, s1, s2 = c_ref[...], s1_ref[...], s2_ref[...]

    q = jnp.dot(h, w_ref[:, :Q_W], preferred_element_type=_F32)
    q_ss = _segment_sumsq(q, ones_ref[...])
    qw = qw_ref[...]
    scale = HEAD_DIM ** -0.5 * LOG2E
    for g in range(Q_W // LANES):
        sl = slice(g * LANES, (g + 1) * LANES)
        q_out[0, :, sl] = (_norm_rope(q[:, sl], q_ss[:, sl], qw, c, s1, s2) * scale).astype(_BF16)

    kv = jnp.dot(h, w_ref[:, Q_W:Q_W + 2 * KV_W], preferred_element_type=_F32)
    k = kv[:, :KV_W]
    k_ss = _segment_sumsq(k, ones_ref[:KV_W, :KV_W])
    k_out[0] = _norm_rope(k, k_ss, kw_ref[...], c, s1, s2).astype(_BF16)
    v_out[0] = kv[:, KV_W:].astype(_BF16)

    o = Q_W + 2 * KV_W
    uu = jnp.dot(h, w_ref[:, o:o + 2 * CONV_CH], preferred_element_type=_F32)
    a_out[0] = uu[:, :CONV_CH] * jax.nn.sigmoid(uu[:, CONV_CH:])

    o += 2 * CONV_CH
    gates = jnp.dot(h, w_ref[:, o:o + 2 * D_MODEL], preferred_element_type=_F32)
    ga_out[0] = jax.nn.sigmoid(gates[:, :D_MODEL]).astype(_BF16)
    gb_out[0] = jax.nn.sigmoid(gates[:, D_MODEL:]).astype(_BF16)


def _attend_prep(kband, vband):
    lane = lax.broadcasted_iota(jnp.int32, kband.shape, 1)
    low = lane < HEAD_DIM
    zero = jnp.zeros_like(kband)
    rows = 2 * PAIRS * WINDOW
    qi = lax.broadcasted_iota(jnp.int32, (rows, WINDOW), 0) & (WINDOW - 1)
    kj = lax.broadcasted_iota(jnp.int32, (rows, WINDOW), 1)
    return dict(
        k=(kband, pltpu.roll(kband, HEAD_DIM, 1)), v=(vband, pltpu.roll(vband, HEAD_DIM, 1)),
        low=low, zero=zero,
        ones_lo=low.astype(_F32).astype(_BF16), ones_hi=(1.0 - low.astype(_F32)).astype(_BF16),
        tri=kj <= qi,
        low_o=lax.broadcasted_iota(jnp.int32, (PAIRS * WINDOW, LANES), 1) < HEAD_DIM)


def _attend_scores(q, prep, g):
    low, zero = prep["low"], prep["zero"]
    k_even = jnp.where(low, prep["k"][g], zero)
    k_odd = jnp.where(low, zero, prep["k"][1 - g])
    base = g * GROUP * HEAD_DIM
    qp = jnp.concatenate([q[:, base + t * LANES: base + (t + 1) * LANES] for t in range(PAIRS)], axis=0)
    contract_last = (((1,), (1,)), ((), ()))
    return jnp.concatenate([lax.dot_general(qp, k_even, contract_last, preferred_element_type=_F32),
                            lax.dot_general(qp, k_odd, contract_last, preferred_element_type=_F32)], axis=0)


def _attend_softmax(s, prep, g, sink_of, first_block):
    tri = prep["tri"]
    sc = jnp.where(tri, s[:, :WINDOW], s[:, WINDOW:])
    if first_block is not None:
        sc = jnp.where(jnp.logical_or(tri, jnp.logical_not(first_block)), sc, -jnp.inf)
    sink = jnp.concatenate(
        [jnp.full((WINDOW, 1), sink_of(g * GROUP + 2 * t + par), _F32)
         for par in range(2) for t in range(PAIRS)], axis=0)
    m = jnp.maximum(jnp.max(sc, axis=-1, keepdims=True), sink)
    p = jnp.exp2(sc - m).astype(_BF16)
    pz = jnp.zeros_like(p)
    pb = jnp.concatenate([jnp.where(tri, p, pz), jnp.where(tri, pz, p)], axis=1)
    return pb, jnp.exp2(sink - m)


def _attend_values(pb, sink_p, prep, g):
    low, zero = prep["low"], prep["zero"]
    half = PAIRS * WINDOW
    w_even = jnp.concatenate([jnp.where(low, prep["v"][g], zero), prep["ones_lo"]], axis=1)
    w_odd = jnp.concatenate([jnp.where(low, zero, prep["v"][1 - g]), prep["ones_hi"]], axis=1)
    wide = (jnp.dot(pb[:half], w_even, preferred_element_type=_F32)
            + jnp.dot(pb[half:], w_odd, preferred_element_type=_F32))
    den = wide[:, LANES:] + jnp.where(prep["low_o"], sink_p[:half], sink_p[half:])
    o = wide[:, :LANES] / den
    return [o[t * WINDOW:(t + 1) * WINDOW] for t in range(PAIRS)]


def _mix_ffn_kernel(sinks_ref, x_ref, q_ref, kc_ref, kp_ref, vc_ref, vp_ref, ac_ref, ah_ref,
                    ga_ref, gb_ref, cw_ref, cb_ref, lg_ref, lb_ref, wpw_ref, wout_ref,
                    gf_ref, wgu_ref, wd_ref,
                    o_ref, aext_ref, shift_ref, x1_ref, *, blocks_per_seq, n_blocks):
    s = pl.program_id(0)
    first = jnp.minimum(s, n_blocks - 1) % blocks_per_seq == 0

    @pl.when(s == 0)
    def _():
        x1_ref[...] = jnp.zeros_like(x1_ref)

    k_all = jnp.concatenate([kp_ref[...], kc_ref[...]], axis=0)
    v_all = jnp.concatenate([vp_ref[...], vc_ref[...]], axis=0)
    sink_of = lambda head: sinks_ref[head] * LOG2E
    blocks = []
    for j in range(NSUB):
        cur = slice((j + 1) * WINDOW, (j + 2) * WINDOW)
        prv = slice(j * WINDOW, (j + 1) * WINDOW)
        prep = _attend_prep(jnp.concatenate([k_all[cur], k_all[prv]], axis=0),
                            jnp.concatenate([v_all[cur], v_all[prv]], axis=0))
        q = q_ref[j * WINDOW:(j + 1) * WINDOW, :]
        slabs = []
        for g in range(N_KV_HEADS):
            sc = _attend_scores(q, prep, g)
            pb, sink_p = _attend_softmax(sc, prep, g, sink_of, first if j == 0 else None)
            slabs.extend(_attend_values(pb, sink_p, prep, g))
        blocks.append(jnp.concatenate(slabs, axis=1))
    a_att = jnp.concatenate(blocks, axis=0)

    aext_ref[:HALO] = jnp.where(first, jnp.zeros_like(ah_ref[...]), ah_ref[...])
    aext_ref[HALO:] = ac_ref[...]
    for r in range(1, SUBLANES):
        shift_ref[r - 1] = aext_ref[r:r + SHIFT_ROWS, :]
    y = jnp.broadcast_to(cb_ref[...], (TM_MIX, CONV_CH))
    for j in range(CONV_WIDTH):
        off = CONV_LEAD + j
        r, base = off % SUBLANES, off - off % SUBLANES
        win = aext_ref[base:base + TM_MIX, :] if r == 0 else shift_ref[r - 1, base:base + TM_MIX, :]
        y = y + cw_ref[j:j + 1, :] * win
    mu = jnp.mean(y, axis=-1, keepdims=True)
    yc = y - mu
    var = jnp.mean(yc * yc, axis=-1, keepdims=True)
    yn = yc * lax.rsqrt(var + EPS) * lg_ref[...] + lb_ref[...]
    cact = (yn * jax.nn.sigmoid(yn)).astype(_BF16)
    c_out = jnp.dot(cact, wpw_ref[...], preferred_element_type=_F32)

    merged = ga_ref[...].astype(_F32) * a_att + gb_ref[...].astype(_F32) * c_out
    x1_new = x_ref[...] + jnp.dot(merged.astype(_BF16), wout_ref[...], preferred_element_type=_F32)

    x1 = x1_ref[...]
    ms = jnp.mean(x1 * x1, axis=-1, keepdims=True)
    h = (x1 * lax.rsqrt(ms + EPS) * gf_ref[...]).astype(_BF16)
    gate = jnp.dot(h, wgu_ref[:, :D_FF], preferred_element_type=_F32)
    up = jnp.dot(h, wgu_ref[:, D_FF:], preferred_element_type=_F32)
    act = (gate * jax.nn.sigmoid(gate) * up).astype(_BF16)
    o_ref[...] = x1_ref[...] + jnp.dot(act, wd_ref[...], preferred_element_type=_F32)

    x1_ref[...] = x1_new


def _resident(shape):
    return pl.BlockSpec(shape, lambda *_: (0,) * len(shape), pipeline_mode=pl.Buffered(1))


def _resident_layer(shape, layer):
    return pl.BlockSpec((None,) + shape, lambda *_: (layer,) + (0,) * len(shape), pipeline_mode=pl.Buffered(1))


def _params():
    return pltpu.CompilerParams(dimension_semantics=("parallel", "parallel"), vmem_limit_bytes=VMEM_LIMIT)


def _proj_call(layer, x, g, w, qw, kw, c, s1, s2, ones_blockdiag):
    B, T, _ = x.shape
    row = lambda width: pl.BlockSpec((1, TM_PROJ, width), lambda b, i: (b, i, 0))
    tab = pl.BlockSpec((TM_PROJ, LANES), lambda b, i: (i, 0))
    out = lambda width, dt: jax.ShapeDtypeStruct((B, T, width), dt)
    return pl.pallas_call(
        _proj_kernel,
        grid=(B, T // TM_PROJ),
        in_specs=[row(D_MODEL), _resident((1, D_MODEL)), _resident_layer((D_MODEL, IN_W), layer),
                  _resident((1, LANES)), _resident((1, LANES)), tab, tab, tab,
                  _resident((MXU_DIM, MXU_DIM))],
        out_specs=[row(Q_W), row(KV_W), row(KV_W), row(CONV_CH), row(D_MODEL), row(D_MODEL)],
        out_shape=[out(Q_W, _BF16), out(KV_W, _BF16), out(KV_W, _BF16), out(CONV_CH, _F32),
                   out(D_MODEL, _BF16), out(D_MODEL, _BF16)],
        compiler_params=_params(),
        name="proj",
    )(x, g, w, qw, kw, c, s1, s2, ones_blockdiag)


def _mix_ffn_call(layer, sinks, x, q, k, v, a, ga, gb, cw, cb, lg, lb, wpw, wout, gf, wgu, wd):
    B, T, _ = x.shape
    flat = lambda t: t.reshape(B * T, t.shape[-1])
    n_blocks = B * T // TM_MIX
    blk = lambda s: jnp.minimum(s, n_blocks - 1)
    row = lambda width: pl.BlockSpec((TM_MIX, width), lambda s: (blk(s), 0))
    prev = pl.BlockSpec((WINDOW, KV_W), lambda s: (jnp.maximum(blk(s) * NSUB - 1, 0), 0))
    halo = pl.BlockSpec((HALO, CONV_CH), lambda s: (jnp.maximum(blk(s) * (TM_MIX // HALO) - 1, 0), 0))
    body = functools.partial(_mix_ffn_kernel, blocks_per_seq=T // TM_MIX, n_blocks=n_blocks)
    out = pl.pallas_call(
        body,
        grid=(n_blocks + 1,),
        in_specs=[pl.BlockSpec(memory_space=pltpu.SMEM),
                  row(D_MODEL), row(Q_W), row(KV_W), prev, row(KV_W), prev, row(CONV_CH), halo,
                  row(D_MODEL), row(D_MODEL),
                  _resident((CONV_WIDTH, CONV_CH)), _resident((1, CONV_CH)), _resident((1, CONV_CH)),
                  _resident((1, CONV_CH)), _resident_layer((CONV_CH, D_MODEL), layer),
                  _resident_layer((D_MODEL, D_MODEL), layer), _resident((1, D_MODEL)),
                  _resident_layer((D_MODEL, 2 * D_FF), layer), _resident_layer((D_FF, D_MODEL), layer)],
        out_specs=pl.BlockSpec((TM_MIX, D_MODEL), lambda s: (jnp.maximum(s - 1, 0), 0)),
        out_shape=jax.ShapeDtypeStruct((B * T, D_MODEL), _F32),
        scratch_shapes=[pltpu.VMEM((HALO + TM_MIX, CONV_CH), _F32),
                        pltpu.VMEM((SUBLANES - 1, SHIFT_ROWS, CONV_CH), _F32),
                        pltpu.VMEM((TM_MIX, D_MODEL), _F32)],
        compiler_params=pltpu.CompilerParams(dimension_semantics=("arbitrary",), vmem_limit_bytes=VMEM_LIMIT),
        name="mix_ffn",
    )(sinks, flat(x), flat(q), flat(k), flat(k), flat(v), flat(v), flat(a), flat(a), flat(ga), flat(gb),
      cw, cb, lg, lb, wpw, wout, gf, wgu, wd)
    return out.reshape(B, T, D_MODEL)


def _rope_lane_tables(seq):
    half = ROT_DIM // 2
    inv_freq = ROPE_THETA ** (-jnp.arange(0, ROT_DIM, 2, dtype=_F32) / ROT_DIM)
    dim = jnp.arange(LANES) % HEAD_DIM
    ang = jnp.arange(seq, dtype=_F32)[:, None] * inv_freq[dim % half][None, :]
    cos, sin = jnp.cos(ang), jnp.sin(ang)
    c = jnp.where(dim < ROT_DIM, cos, 1.0)
    s1 = jnp.where((dim >= half) & (dim < ROT_DIM), sin, 0.0)
    s2 = jnp.where(dim < half, -sin, 0.0)
    return c, s1, s2


def kernel(x, norm_mix, w_in, q_norm, k_norm, sinks, conv_w, conv_b, conv_ln_g, conv_ln_b,
           w_conv_out, w_out, norm_ffn, w_gate_up, w_down):
    B, T, D = x.shape
    depth = w_in.shape[0]
    assert D == D_MODEL and T % TM_PROJ == 0 and T % TM_MIX == 0
    c, s1, s2 = _rope_lane_tables(T)
    seg = jnp.arange(MXU_DIM) // HEAD_DIM
    ones_blockdiag = (seg[:, None] == seg[None, :]).astype(_BF16)
    dup = lambda t: jnp.concatenate([t, t], axis=-1)[None, :]
    w_in, w_conv_out, w_out, w_gate_up, w_down = (
        w.astype(_BF16) for w in (w_in, w_conv_out, w_out, w_gate_up, w_down))
    for l in range(depth):
        q, k, v, a, ga, gb = _proj_call(
            l, x, norm_mix[l][None, :], w_in, dup(q_norm[l]), dup(k_norm[l]), c, s1, s2, ones_blockdiag)
        x = _mix_ffn_call(l, sinks[l], x, q, k, v, a, ga, gb, conv_w[l], conv_b[l][None, :],
                          conv_ln_g[l][None, :], conv_ln_b[l][None, :], w_conv_out, w_out,
                          norm_ffn[l][None, :], w_gate_up, w_down)
    return x
```

```python
import functools
import math

import jax
import jax.numpy as jnp
from jax import lax
from jax.experimental import pallas as pl
from jax.experimental.pallas import tpu as pltpu

D_MODEL = 1024
N_HEADS = 16
N_KV_HEADS = 2
HEAD_DIM = 64
GROUP = N_HEADS // N_KV_HEADS
ROT_DIM = HEAD_DIM // 4
ROPE_THETA = 500000.0
WINDOW = 128
CONV_CH = D_MODEL // 2
CONV_WIDTH = 31
D_FF = -(-(8 * D_MODEL) // (3 * 256)) * 256
EPS = 1e-6
Q_W = N_HEADS * HEAD_DIM
KV_W = N_KV_HEADS * HEAD_DIM
IN_W = Q_W + 2 * KV_W + 2 * CONV_CH + 2 * D_MODEL

LANES = 128
SUBLANES = 8
MXU_DIM = 256
HALO = 32
PAIRS = GROUP // 2
VMEM_LIMIT = 56 * 1024 * 1024
LOG2E = math.log2(math.e)

TM_PROJ = 1024
NSUB = 2
TM_MIX = NSUB * WINDOW
CONV_LEAD = HALO - (CONV_WIDTH - 1)
SHIFT_ROWS = TM_MIX + HALO - SUBLANES

_F32 = jnp.float32
_BF16 = jnp.bfloat16


def _segment_sumsq(x, ones_blockdiag):
    xx = x * x
    hi = xx.astype(_BF16)
    lo = (xx - hi.astype(_F32)).astype(_BF16)
    width = ones_blockdiag.shape[0]
    parts = []
    for c in range(x.shape[1] // width):
        sl = slice(c * width, (c + 1) * width)
        parts.append(jnp.dot(hi[:, sl], ones_blockdiag, preferred_element_type=_F32)
                     + jnp.dot(lo[:, sl], ones_blockdiag, preferred_element_type=_F32))
    return parts[0] if len(parts) == 1 else jnp.concatenate(parts, axis=1)


def _norm_rope(xg, ss, w, c, s1, s2):
    xn = xg * lax.rsqrt(ss * (1.0 / HEAD_DIM) + EPS) * w
    return xn * c + pltpu.roll(xn, ROT_DIM // 2, 1) * s1 + pltpu.roll(xn, LANES - ROT_DIM // 2, 1) * s2


def _proj_kernel(x_ref, g_ref, w_ref, qw_ref, kw_ref, c_ref, s1_ref, s2_ref, ones_ref,
                 q_out, k_out, v_out, a_out, ga_out, gb_out):
    x = x_ref[0]
    ms = jnp.mean(x * x, axis=-1, keepdims=True)
    h = (x * lax.rsqrt(ms + EPS) * g_ref[...]).astype(_BF16)
    c, s1, s2 = c_ref[...], s1_ref[...], s2_ref[...]

    q = jnp.dot(h, w_ref[:, :Q_W], preferred_element_type=_F32)
    q_ss = _segment_sumsq(q, ones_ref[...])
    qw = qw_ref[...]
    scale = HEAD_DIM ** -0.5 * LOG2E
    for g in range(Q_W // LANES):
        sl = slice(g * LANES, (g + 1) * LANES)
        q_out[0, :, sl] = (_norm_rope(q[:, sl], q_ss[:, sl], qw, c, s1, s2) * scale).astype(_BF16)

    kv = jnp.dot(h, w_ref[:, Q_W:Q_W + 2 * KV_W], preferred_element_type=_F32)
    k = kv[:, :KV_W]
    k_ss = _segment_sumsq(k, ones_ref[:KV_W, :KV_W])
    k_out[0] = _norm_rope(k, k_ss, kw_ref[...], c, s1, s2).astype(_BF16)
    v_out[0] = kv[:, KV_W:].astype(_BF16)

    o = Q_W + 2 * KV_W
    uu = jnp.dot(h, w_ref[:, o:o + 2 * CONV_CH], preferred_element_type=_F32)
    a_out[0] = uu[:, :CONV_CH] * jax.nn.sigmoid(uu[:, CONV_CH:])

    o += 2 * CONV_CH
    gates = jnp.dot(h, w_ref[:, o:o + 2 * D_MODEL], preferred_element_type=_F32)
    ga_out[0] = jax.nn.sigmoid(gates[:, :D_MODEL]).astype(_BF16)
    gb_out[0] = jax.nn.sigmoid(gates[:, D_MODEL:]).astype(_BF16)


def _attend_prep(kband, vband):
    lane = lax.broadcasted_iota(jnp.int32, kband.shape, 1)
    low = lane < HEAD_DIM
    zero = jnp.zeros_like(kband)
    rows = 2 * PAIRS * WINDOW
    qi = lax.broadcasted_iota(jnp.int32, (rows, WINDOW), 0) & (WINDOW - 1)
    kj = lax.broadcasted_iota(jnp.int32, (rows, WINDOW), 1)
    return dict(
        k=(kband, pltpu.roll(kband, HEAD_DIM, 1)), v=(vband, pltpu.roll(vband, HEAD_DIM, 1)),
        low=low, zero=zero,
        ones_lo=low.astype(_F32).astype(_BF16), ones_hi=(1.0 - low.astype(_F32)).astype(_BF16),
        tri=kj <= qi,
        low_o=lax.broadcasted_iota(jnp.int32, (PAIRS * WINDOW, LANES), 1) < HEAD_DIM)


def _attend_scores(q, prep, g):
    low, zero = prep["low"], prep["zero"]
    k_even = jnp.where(low, prep["k"][g], zero)
    k_odd = jnp.where(low, zero, prep["k"][1 - g])
    base = g * GROUP * HEAD_DIM
    qp = jnp.concatenate([q[:, base + t * LANES: base + (t + 1) * LANES] for t in range(PAIRS)], axis=0)
    contract_last = (((1,), (1,)), ((), ()))
    return jnp.concatenate([lax.dot_general(qp, k_even, contract_last, preferred_element_type=_F32),
                            lax.dot_general(qp, k_odd, contract_last, preferred_element_type=_F32)], axis=0)


def _attend_softmax(s, prep, g, sink_of, first_block):
    tri = prep["tri"]
    sc = jnp.where(tri, s[:, :WINDOW], s[:, WINDOW:])
    if first_block is not None:
        sc = jnp.where(jnp.logical_or(tri, jnp.logical_not(first_block)), sc, -jnp.inf)
    ps, sink_ps = [], []
    for i, (par, t) in enumerate((par, t) for par in range(2) for t in range(PAIRS)):
        sink = sink_of(g * GROUP + 2 * t + par)
        sc_h = sc[i * WINDOW:(i + 1) * WINDOW]
        m = jnp.maximum(jnp.max(sc_h, axis=-1, keepdims=True), sink)
        ps.append(jnp.exp2(sc_h - m).astype(_BF16))
        sink_ps.append(jnp.exp2(sink - m))
    p = jnp.concatenate(ps, axis=0)
    pz = jnp.zeros_like(p)
    pb = jnp.concatenate([jnp.where(tri, p, pz), jnp.where(tri, pz, p)], axis=1)
    return pb, jnp.concatenate(sink_ps, axis=0)


def _attend_values(pb, sink_p, prep, g):
    low, zero = prep["low"], prep["zero"]
    half = PAIRS * WINDOW
    w_even = jnp.concatenate([jnp.where(low, prep["v"][g], zero), prep["ones_lo"]], axis=1)
    w_odd = jnp.concatenate([jnp.where(low, zero, prep["v"][1 - g]), prep["ones_hi"]], axis=1)
    wide = (jnp.dot(pb[:half], w_even, preferred_element_type=_F32)
            + jnp.dot(pb[half:], w_odd, preferred_element_type=_F32))
    den = wide[:, LANES:] + jnp.where(prep["low_o"], sink_p[:half], sink_p[half:])
    o = wide[:, :LANES] / den
    return [o[t * WINDOW:(t + 1) * WINDOW] for t in range(PAIRS)]


def _mix_ffn_kernel(sinks_ref, x_ref, q_ref, kc_ref, kp_ref, vc_ref, vp_ref, ac_ref, ah_ref,
                    ga_ref, gb_ref, cw_ref, cb_ref, lg_ref, lb_ref, wpw_ref, wout_ref,
                    gf_ref, wgu_ref, wd_ref,
                    o_ref, aext_ref, shift_ref, x1_ref, *, blocks_per_seq, n_blocks):
    s = pl.program_id(0)
    first = jnp.minimum(s, n_blocks - 1) % blocks_per_seq == 0

    @pl.when(s == 0)
    def _():
        x1_ref[...] = jnp.zeros_like(x1_ref)

    k_all = jnp.concatenate([kp_ref[...], kc_ref[...]], axis=0)
    v_all = jnp.concatenate([vp_ref[...], vc_ref[...]], axis=0)
    sink_of = lambda head: sinks_ref[head] * LOG2E
    blocks = []
    for j in range(NSUB):
        cur = slice((j + 1) * WINDOW, (j + 2) * WINDOW)
        prv = slice(j * WINDOW, (j + 1) * WINDOW)
        prep = _attend_prep(jnp.concatenate([k_all[cur], k_all[prv]], axis=0),
                            jnp.concatenate([v_all[cur], v_all[prv]], axis=0))
        q = q_ref[j * WINDOW:(j + 1) * WINDOW, :]
        slabs = []
        for g in range(N_KV_HEADS):
            sc = _attend_scores(q, prep, g)
            pb, sink_p = _attend_softmax(sc, prep, g, sink_of, first if j == 0 else None)
            slabs.extend(_attend_values(pb, sink_p, prep, g))
        blocks.append(jnp.concatenate(slabs, axis=1))
    a_att = jnp.concatenate(blocks, axis=0)

    aext_ref[:HALO] = jnp.where(first, jnp.zeros_like(ah_ref[...]), ah_ref[...])
    aext_ref[HALO:] = ac_ref[...]
    for r in range(1, SUBLANES):
        shift_ref[r - 1] = aext_ref[r:r + SHIFT_ROWS, :]
    y = jnp.broadcast_to(cb_ref[...], (TM_MIX, CONV_CH))
    for j in range(CONV_WIDTH):
        off = CONV_LEAD + j
        r, base = off % SUBLANES, off - off % SUBLANES
        win = aext_ref[base:base + TM_MIX, :] if r == 0 else shift_ref[r - 1, base:base + TM_MIX, :]
        y = y + cw_ref[j:j + 1, :] * win
    mu = jnp.mean(y, axis=-1, keepdims=True)
    yc = y - mu
    var = jnp.mean(yc * yc, axis=-1, keepdims=True)
    yn = yc * lax.rsqrt(var + EPS) * lg_ref[...] + lb_ref[...]
    cact = (yn * jax.nn.sigmoid(yn)).astype(_BF16)
    c_out = jnp.dot(cact, wpw_ref[...], preferred_element_type=_F32)

    merged = ga_ref[...].astype(_F32) * a_att + gb_ref[...].astype(_F32) * c_out
    x1_new = x_ref[...] + jnp.dot(merged.astype(_BF16), wout_ref[...], preferred_element_type=_F32)

    x1 = x1_ref[...]
    ms = jnp.mean(x1 * x1, axis=-1, keepdims=True)
    h = (x1 * lax.rsqrt(ms + EPS) * gf_ref[...]).astype(_BF16)
    gate = jnp.dot(h, wgu_ref[:, :D_FF], preferred_element_type=_F32)
    up = jnp.dot(h, wgu_ref[:, D_FF:], preferred_element_type=_F32)
    act = (gate * jax.nn.sigmoid(gate) * up).astype(_BF16)
    o_ref[...] = x1_ref[...] + jnp.dot(act, wd_ref[...], preferred_element_type=_F32)

    x1_ref[...] = x1_new


def _resident(shape):
    return pl.BlockSpec(shape, lambda *_: (0,) * len(shape), pipeline_mode=pl.Buffered(1))


def _resident_layer(shape, layer):
    return pl.BlockSpec((None,) + shape, lambda *_: (layer,) + (0,) * len(shape), pipeline_mode=pl.Buffered(1))


def _params():
    return pltpu.CompilerParams(dimension_semantics=("parallel", "parallel"), vmem_limit_bytes=VMEM_LIMIT)


def _proj_call(layer, x, g, w, qw, kw, c, s1, s2, ones_blockdiag):
    B, T, _ = x.shape
    row = lambda width: pl.BlockSpec((1, TM_PROJ, width), lambda b, i: (b, i, 0))
    tab = pl.BlockSpec((TM_PROJ, LANES), lambda b, i: (i, 0))
    out = lambda width, dt: jax.ShapeDtypeStruct((B, T, width), dt)
    return pl.pallas_call(
        _proj_kernel,
        grid=(B, T // TM_PROJ),
        in_specs=[row(D_MODEL), _resident((1, D_MODEL)), _resident_layer((D_MODEL, IN_W), layer),
                  _resident((1, LANES)), _resident((1, LANES)), tab, tab, tab,
                  _resident((MXU_DIM, MXU_DIM))],
        out_specs=[row(Q_W), row(KV_W), row(KV_W), row(CONV_CH), row(D_MODEL), row(D_MODEL)],
        out_shape=[out(Q_W, _BF16), out(KV_W, _BF16), out(KV_W, _BF16), out(CONV_CH, _F32),
                   out(D_MODEL, _BF16), out(D_MODEL, _BF16)],
        compiler_params=_params(),
        name="proj",
    )(x, g, w, qw, kw, c, s1, s2, ones_blockdiag)


def _mix_ffn_call(layer, sinks, x, q, k, v, a, ga, gb, cw, cb, lg, lb, wpw, wout, gf, wgu, wd):
    B, T, _ = x.shape
    flat = lambda t: t.reshape(B * T, t.shape[-1])
    n_blocks = B * T // TM_MIX
    blk = lambda s: jnp.minimum(s, n_blocks - 1)
    row = lambda width: pl.BlockSpec((TM_MIX, width), lambda s: (blk(s), 0))
    prev = pl.BlockSpec((WINDOW, KV_W), lambda s: (jnp.maximum(blk(s) * NSUB - 1, 0), 0))
    halo = pl.BlockSpec((HALO, CONV_CH), lambda s: (jnp.maximum(blk(s) * (TM_MIX // HALO) - 1, 0), 0))
    body = functools.partial(_mix_ffn_kernel, blocks_per_seq=T // TM_MIX, n_blocks=n_blocks)
    out = pl.pallas_call(
        body,
        grid=(n_blocks + 1,),
        in_specs=[pl.BlockSpec(memory_space=pltpu.SMEM),
                  row(D_MODEL), row(Q_W), row(KV_W), prev, row(KV_W), prev, row(CONV_CH), halo,
                  row(D_MODEL), row(D_MODEL),
                  _resident((CONV_WIDTH, CONV_CH)), _resident((1, CONV_CH)), _resident((1, CONV_CH)),
                  _resident((1, CONV_CH)), _resident_layer((CONV_CH, D_MODEL), layer),
                  _resident_layer((D_MODEL, D_MODEL), layer), _resident((1, D_MODEL)),
                  _resident_layer((D_MODEL, 2 * D_FF), layer), _resident_layer((D_FF, D_MODEL), layer)],
        out_specs=pl.BlockSpec((TM_MIX, D_MODEL), lambda s: (jnp.maximum(s - 1, 0), 0)),
        out_shape=jax.ShapeDtypeStruct((B * T, D_MODEL), _F32),
        scratch_shapes=[pltpu.VMEM((HALO + TM_MIX, CONV_CH), _F32),
                        pltpu.VMEM((SUBLANES - 1, SHIFT_ROWS, CONV_CH), _F32),
                        pltpu.VMEM((TM_MIX, D_MODEL), _F32)],
        compiler_params=pltpu.CompilerParams(dimension_semantics=("arbitrary",), vmem_limit_bytes=VMEM_LIMIT),
        name="mix_ffn",
    )(sinks, flat(x), flat(q), flat(k), flat(k), flat(v), flat(v), flat(a), flat(a), flat(ga), flat(gb),
      cw, cb, lg, lb, wpw, wout, gf, wgu, wd)
    return out.reshape(B, T, D_MODEL)


def _rope_lane_tables(seq):
    half = ROT_DIM // 2
    inv_freq = ROPE_THETA ** (-jnp.arange(0, ROT_DIM, 2, dtype=_F32) / ROT_DIM)
    dim = jnp.arange(LANES) % HEAD_DIM
    ang = jnp.arange(seq, dtype=_F32)[:, None] * inv_freq[dim % half][None, :]
    cos, sin = jnp.cos(ang), jnp.sin(ang)
    c = jnp.where(dim < ROT_DIM, cos, 1.0)
    s1 = jnp.where((dim >= half) & (dim < ROT_DIM), sin, 0.0)
    s2 = jnp.where(dim < half, -sin, 0.0)
    return c, s1, s2


def kernel(x, norm_mix, w_in, q_norm, k_norm, sinks, conv_w, conv_b, conv_ln_g, conv_ln_b,
           w_conv_out, w_out, norm_ffn, w_gate_up, w_down):
    B, T, D = x.shape
    depth = w_in.shape[0]
    assert D == D_MODEL and T % TM_PROJ == 0 and T % TM_MIX == 0
    c, s1, s2 = _rope_lane_tables(T)
    seg = jnp.arange(MXU_DIM) // HEAD_DIM
    ones_blockdiag = (seg[:, None] == seg[None, :]).astype(_BF16)
    dup = lambda t: jnp.concatenate([t, t], axis=-1)[None, :]
    w_in, w_conv_out, w_out, w_gate_up, w_down = (
        w.astype(_BF16) for w in (w_in, w_conv_out, w_out, w_gate_up, w_down))
    for l in range(depth):
        q, k, v, a, ga, gb = _proj_call(
            l, x, norm_mix[l][None, :], w_in, dup(q_norm[l]), dup(k_norm[l]), c, s1, s2, ones_blockdiag)
        x = _mix_ffn_call(l, sinks[l], x, q, k, v, a, ga, gb, conv_w[l], conv_b[l][None, :],
                          conv_ln_g[l][None, :], conv_ln_b[l][None, :], w_conv_out, w_out,
                          norm_ffn[l][None, :], w_gate_up, w_down)
    return x
```

```python
import functools
import math

import jax
import jax.numpy as jnp
from jax import lax
from jax.experimental import pallas as pl
from jax.experimental.pallas import tpu as pltpu

D_MODEL = 1024
N_HEADS = 16
N_KV_HEADS = 2
HEAD_DIM = 64
GROUP = N_HEADS // N_KV_HEADS
ROT_DIM = HEAD_DIM // 4
ROPE_THETA = 500000.0
WINDOW = 128
CONV_CH = D_MODEL // 2
CONV_WIDTH = 31
D_FF = -(-(8 * D_MODEL) // (3 * 256)) * 256
EPS = 1e-6
Q_W = N_HEADS * HEAD_DIM
KV_W = N_KV_HEADS * HEAD_DIM
IN_W = Q_W + 2 * KV_W + 2 * CONV_CH + 2 * D_MODEL

LANES = 128
SUBLANES = 8
MXU_DIM = 256
HALO = 32
PAIRS = GROUP // 2
VMEM_LIMIT = 56 * 1024 * 1024
LOG2E = math.log2(math.e)

TM_PROJ = 1024
NSUB = 2
TM_MIX = NSUB * WINDOW
CONV_LEAD = HALO - (CONV_WIDTH - 1)
SHIFT_ROWS = TM_MIX + HALO - SUBLANES

_F32 = jnp.float32
_BF16 = jnp.bfloat16


def _segment_sumsq(x, ones_blockdiag):
    xx = x * x
    hi = xx.astype(_BF16)
    lo = (xx - hi.astype(_F32)).astype(_BF16)
    width = ones_blockdiag.shape[0]
    parts = []
    for c in range(x.shape[1] // width):
        sl = slice(c * width, (c + 1) * width)
        parts.append(jnp.dot(hi[:, sl], ones_blockdiag, preferred_element_type=_F32)
                     + jnp.dot(lo[:, sl], ones_blockdiag, preferred_element_type=_F32))
    return parts[0] if len(parts) == 1 else jnp.concatenate(parts, axis=1)


def _norm_rope(xg, ss, w, c, s1, s2):
    xn = xg * lax.rsqrt(ss * (1.0 / HEAD_DIM) + EPS) * w
    return xn * c + pltpu.roll(xn, ROT_DIM // 2, 1) * s1 + pltpu.roll(xn, LANES - ROT_DIM // 2, 1) * s2


def _proj_kernel(x_ref, g_ref, w_ref, qw_ref, kw_ref, c_ref, s1_ref, s2_ref, ones_ref,
                 q_out, k_out, v_out, a_out, ga_out, gb_out):
    x = x_ref[0]
    ms = jnp.mean(x * x, axis=-1, keepdims=True)
    h = (x * lax.rsqrt(ms + EPS) * g_ref[...]).astype(_BF16)
    c, s1, s2 = c_ref[...], s1_ref[...], s2_ref[...]

    q = jnp.dot(h, w_ref[:, :Q_W], preferred_element_type=_F32)
    q_ss = _segment_sumsq(q, ones_ref[...])
    qw = qw_ref[...]
    scale = HEAD_DIM ** -0.5 * LOG2E
    for g in range(Q_W // LANES):
        sl = slice(g * LANES, (g + 1) * LANES)
        q_out[0, :, sl] = (_norm_rope(q[:, sl], q_ss[:, sl], qw, c, s1, s2) * scale).astype(_BF16)

    kv = jnp.dot(h, w_ref[:, Q_W:Q_W + 2 * KV_W], preferred_element_type=_F32)
    k = kv[:, :KV_W]
    k_ss = _segment_sumsq(k, ones_ref[:KV_W, :KV_W])
    k_out[0] = _norm_rope(k, k_ss, kw_ref[...], c, s1, s2).astype(_BF16)
    v_out[0] = kv[:, KV_W:].astype(_BF16)

    o = Q_W + 2 * KV_W
    uu = jnp.dot(h, w_ref[:, o:o + 2 * CONV_CH], preferred_element_type=_F32)
    a_out[0] = uu[:, :CONV_CH] * jax.nn.sigmoid(uu[:, CONV_CH:])

    o += 2 * CONV_CH
    gates = jnp.dot(h, w_ref[:, o:o + 2 * D_MODEL], preferred_element_type=_F32)
    ga_out[0] = jax.nn.sigmoid(gates[:, :D_MODEL]).astype(_BF16)
    gb_out[0] = jax.nn.sigmoid(gates[:, D_MODEL:]).astype(_BF16)


def _attend_prep(kband, vband):
    lane = lax.broadcasted_iota(jnp.int32, kband.shape, 1)
    low = lane < HEAD_DIM
    zero = jnp.zeros_like(kband)
    rows = 2 * PAIRS * WINDOW
    qi = lax.broadcasted_iota(jnp.int32, (rows, WINDOW), 0) & (WINDOW - 1)
    kj = lax.broadcasted_iota(jnp.int32, (rows, WINDOW), 1)
    return dict(
        k=(kband, pltpu.roll(kband, HEAD_DIM, 1)), v=(vband, pltpu.roll(vband, HEAD_DIM, 1)),
        low=low, zero=zero,
        ones_lo=low.astype(_F32).astype(_BF16), ones_hi=(1.0 - low.astype(_F32)).astype(_BF16),
        tri=kj <= qi,
        low_o=lax.broadcasted_iota(jnp.int32, (PAIRS * WINDOW, LANES), 1) < HEAD_DIM)


def _attend_scores(q, prep, g):
    low, zero = prep["low"], prep["zero"]
    k_even = jnp.where(low, prep["k"][g], zero)
    k_odd = jnp.where(low, zero, prep["k"][1 - g])
    base = g * GROUP * HEAD_DIM
    qp = jnp.concatenate([q[:, base + t * LANES: base + (t + 1) * LANES] for t in range(PAIRS)], axis=0)
    contract_last = (((1,), (1,)), ((), ()))
    return jnp.concatenate([lax.dot_general(qp, k_even, contract_last, preferred_element_type=_F32),
                            lax.dot_general(qp, k_odd, contract_last, preferred_element_type=_F32)], axis=0)


def _attend_softmax(s, prep, g, sink_of, first_block):
    tri = prep["tri"]
    sc = jnp.where(tri, s[:, :WINDOW], s[:, WINDOW:])
    if first_block is not None:
        sc = jnp.where(jnp.logical_or(tri, jnp.logical_not(first_block)), sc, -jnp.inf)
    ps, sink_ps = [], []
    for i, (par, t) in enumerate((par, t) for par in range(2) for t in range(PAIRS)):
        sink = sink_of(g * GROUP + 2 * t + par)
        sc_h = sc[i * WINDOW:(i + 1) * WINDOW]
        m = jnp.maximum(jnp.max(sc_h, axis=-1, keepdims=True), sink)
        ps.append(jnp.exp2(sc_h - m).astype(_BF16))
        sink_ps.append(jnp.exp2(sink - m))
    p = jnp.concatenate(ps, axis=0)
    pz = jnp.zeros_like(p)
    pb = jnp.concatenate([jnp.where(tri, p, pz), jnp.where(tri, pz, p)], axis=1)
    low128 = lax.broadcasted_iota(jnp.int32, (WINDOW, LANES), 1) < HEAD_DIM
    sink_term = jnp.concatenate([jnp.where(low128, sink_ps[t], sink_ps[PAIRS + t]) for t in range(PAIRS)], axis=0)
    return pb, sink_term


def _attend_values(pb, sink_p, prep, g):
    low, zero = prep["low"], prep["zero"]
    half = PAIRS * WINDOW
    w_even = jnp.concatenate([jnp.where(low, prep["v"][g], zero), prep["ones_lo"]], axis=1)
    w_odd = jnp.concatenate([jnp.where(low, zero, prep["v"][1 - g]), prep["ones_hi"]], axis=1)
    wide = (jnp.dot(pb[:half], w_even, preferred_element_type=_F32)
            + jnp.dot(pb[half:], w_odd, preferred_element_type=_F32))
    den = wide[:, LANES:] + sink_p
    o = wide[:, :LANES] / den
    return [o[t * WINDOW:(t + 1) * WINDOW] for t in range(PAIRS)]


def _mix_ffn_kernel(sinks_ref, x_ref, q_ref, kc_ref, kp_ref, vc_ref, vp_ref, ac_ref, ah_ref,
                    ga_ref, gb_ref, cw_ref, cb_ref, lg_ref, lb_ref, wpw_ref, wout_ref,
                    gf_ref, wgu_ref, wd_ref,
                    o_ref, aext_ref, shift_ref, x1_ref, *, blocks_per_seq, n_blocks):
    s = pl.program_id(0)
    first = jnp.minimum(s, n_blocks - 1) % blocks_per_seq == 0

    @pl.when(s == 0)
    def _():
        x1_ref[...] = jnp.zeros_like(x1_ref)

    k_all = jnp.concatenate([kp_ref[...], kc_ref[...]], axis=0)
    v_all = jnp.concatenate([vp_ref[...], vc_ref[...]], axis=0)
    sink_of = lambda head: sinks_ref[head] * LOG2E
    blocks = []
    for j in range(NSUB):
        cur = slice((j + 1) * WINDOW, (j + 2) * WINDOW)
        prv = slice(j * WINDOW, (j + 1) * WINDOW)
        prep = _attend_prep(jnp.concatenate([k_all[cur], k_all[prv]], axis=0),
                            jnp.concatenate([v_all[cur], v_all[prv]], axis=0))
        q = q_ref[j * WINDOW:(j + 1) * WINDOW, :]
        slabs = []
        for g in range(N_KV_HEADS):
            sc = _attend_scores(q, prep, g)
            pb, sink_p = _attend_softmax(sc, prep, g, sink_of, first if j == 0 else None)
            slabs.extend(_attend_values(pb, sink_p, prep, g))
        blocks.append(jnp.concatenate(slabs, axis=1))
    a_att = jnp.concatenate(blocks, axis=0)

    aext_ref[:HALO] = jnp.where(first, jnp.zeros_like(ah_ref[...]), ah_ref[...])
    aext_ref[HALO:] = ac_ref[...]
    for r in range(1, SUBLANES):
        shift_ref[r - 1] = aext_ref[r:r + SHIFT_ROWS, :]
    y = jnp.broadcast_to(cb_ref[...], (TM_MIX, CONV_CH))
    for j in range(CONV_WIDTH):
        off = CONV_LEAD + j
        r, base = off % SUBLANES, off - off % SUBLANES
        win = aext_ref[base:base + TM_MIX, :] if r == 0 else shift_ref[r - 1, base:base + TM_MIX, :]
        y = y + cw_ref[j:j + 1, :] * win
    mu = jnp.mean(y, axis=-1, keepdims=True)
    yc = y - mu
    var = jnp.mean(yc * yc, axis=-1, keepdims=True)
    yn = yc * lax.rsqrt(var + EPS) * lg_ref[...] + lb_ref[...]
    cact = (yn * jax.nn.sigmoid(yn)).astype(_BF16)
    c_out = jnp.dot(cact, wpw_ref[...], preferred_element_type=_F32)

    merged = ga_ref[...].astype(_F32) * a_att + gb_ref[...].astype(_F32) * c_out
    x1_new = x_ref[...] + jnp.dot(merged.astype(_BF16), wout_ref[...], preferred_element_type=_F32)

    x1 = x1_ref[...]
    ms = jnp.mean(x1 * x1, axis=-1, keepdims=True)
    h = (x1 * lax.rsqrt(ms + EPS) * gf_ref[...]).astype(_BF16)
    gate = jnp.dot(h, wgu_ref[:, :D_FF], preferred_element_type=_F32)
    up = jnp.dot(h, wgu_ref[:, D_FF:], preferred_element_type=_F32)
    act = (gate * jax.nn.sigmoid(gate) * up).astype(_BF16)
    o_ref[...] = x1_ref[...] + jnp.dot(act, wd_ref[...], preferred_element_type=_F32)

    x1_ref[...] = x1_new


def _resident(shape):
    return pl.BlockSpec(shape, lambda *_: (0,) * len(shape), pipeline_mode=pl.Buffered(1))


def _resident_layer(shape, layer):
    return pl.BlockSpec((None,) + shape, lambda *_: (layer,) + (0,) * len(shape), pipeline_mode=pl.Buffered(1))


def _params():
    return pltpu.CompilerParams(dimension_semantics=("parallel", "parallel"), vmem_limit_bytes=VMEM_LIMIT)


def _proj_call(layer, x, g, w, qw, kw, c, s1, s2, ones_blockdiag):
    B, T, _ = x.shape
    row = lambda width: pl.BlockSpec((1, TM_PROJ, width), lambda b, i: (b, i, 0))
    tab = pl.BlockSpec((TM_PROJ, LANES), lambda b, i: (i, 0))
    out = lambda width, dt: jax.ShapeDtypeStruct((B, T, width), dt)
    return pl.pallas_call(
        _proj_kernel,
        grid=(B, T // TM_PROJ),
        in_specs=[row(D_MODEL), _resident((1, D_MODEL)), _resident_layer((D_MODEL, IN_W), layer),
                  _resident((1, LANES)), _resident((1, LANES)), tab, tab, tab,
                  _resident((MXU_DIM, MXU_DIM))],
        out_specs=[row(Q_W), row(KV_W), row(KV_W), row(CONV_CH), row(D_MODEL), row(D_MODEL)],
        out_shape=[out(Q_W, _BF16), out(KV_W, _BF16), out(KV_W, _BF16), out(CONV_CH, _F32),
                   out(D_MODEL, _BF16), out(D_MODEL, _BF16)],
        compiler_params=_params(),
        name="proj",
    )(x, g, w, qw, kw, c, s1, s2, ones_blockdiag)


def _mix_ffn_call(layer, sinks, x, q, k, v, a, ga, gb, cw, cb, lg, lb, wpw, wout, gf, wgu, wd):
    B, T, _ = x.shape
    flat = lambda t: t.reshape(B * T, t.shape[-1])
    n_blocks = B * T // TM_MIX
    blk = lambda s: jnp.minimum(s, n_blocks - 1)
    row = lambda width: pl.BlockSpec((TM_MIX, width), lambda s: (blk(s), 0))
    prev = pl.BlockSpec((WINDOW, KV_W), lambda s: (jnp.maximum(blk(s) * NSUB - 1, 0), 0))
    halo = pl.BlockSpec((HALO, CONV_CH), lambda s: (jnp.maximum(blk(s) * (TM_MIX // HALO) - 1, 0), 0))
    body = functools.partial(_mix_ffn_kernel, blocks_per_seq=T // TM_MIX, n_blocks=n_blocks)
    out = pl.pallas_call(
        body,
        grid=(n_blocks + 1,),
        in_specs=[pl.BlockSpec(memory_space=pltpu.SMEM),
                  row(D_MODEL), row(Q_W), row(KV_W), prev, row(KV_W), prev, row(CONV_CH), halo,
                  row(D_MODEL), row(D_MODEL),
                  _resident((CONV_WIDTH, CONV_CH)), _resident((1, CONV_CH)), _resident((1, CONV_CH)),
                  _resident((1, CONV_CH)), _resident_layer((CONV_CH, D_MODEL), layer),
                  _resident_layer((D_MODEL, D_MODEL), layer), _resident((1, D_MODEL)),
                  _resident_layer((D_MODEL, 2 * D_FF), layer), _resident_layer((D_FF, D_MODEL), layer)],
        out_specs=pl.BlockSpec((TM_MIX, D_MODEL), lambda s: (jnp.maximum(s - 1, 0), 0)),
        out_shape=jax.ShapeDtypeStruct((B * T, D_MODEL), _F32),
        scratch_shapes=[pltpu.VMEM((HALO + TM_MIX, CONV_CH), _F32),
                        pltpu.VMEM((SUBLANES - 1, SHIFT_ROWS, CONV_CH), _F32),
                        pltpu.VMEM((TM_MIX, D_MODEL), _F32)],
        compiler_params=pltpu.CompilerParams(dimension_semantics=("arbitrary",), vmem_limit_bytes=VMEM_LIMIT),
        name="mix_ffn",
    )(sinks, flat(x), flat(q), flat(k), flat(k), flat(v), flat(v), flat(a), flat(a), flat(ga), flat(gb),
      cw, cb, lg, lb, wpw, wout, gf, wgu, wd)
    return out.reshape(B, T, D_MODEL)


def _rope_lane_tables(seq):
    half = ROT_DIM // 2
    inv_freq = ROPE_THETA ** (-jnp.arange(0, ROT_DIM, 2, dtype=_F32) / ROT_DIM)
    dim = jnp.arange(LANES) % HEAD_DIM
    ang = jnp.arange(seq, dtype=_F32)[:, None] * inv_freq[dim % half][None, :]
    cos, sin = jnp.cos(ang), jnp.sin(ang)
    c = jnp.where(dim < ROT_DIM, cos, 1.0)
    s1 = jnp.where((dim >= half) & (dim < ROT_DIM), sin, 0.0)
    s2 = jnp.where(dim < half, -sin, 0.0)
    return c, s1, s2


def kernel(x, norm_mix, w_in, q_norm, k_norm, sinks, conv_w, conv_b, conv_ln_g, conv_ln_b,
           w_conv_out, w_out, norm_ffn, w_gate_up, w_down):
    B, T, D = x.shape
    depth = w_in.shape[0]
    assert D == D_MODEL and T % TM_PROJ == 0 and T % TM_MIX == 0
    c, s1, s2 = _rope_lane_tables(T)
    seg = jnp.arange(MXU_DIM) // HEAD_DIM
    ones_blockdiag = (seg[:, None] == seg[None, :]).astype(_BF16)
    dup = lambda t: jnp.concatenate([t, t], axis=-1)[None, :]
    w_in, w_conv_out, w_out, w_gate_up, w_down = (
        w.astype(_BF16) for w in (w_in, w_conv_out, w_out, w_gate_up, w_down))
    for l in range(depth):
        q, k, v, a, ga, gb = _proj_call(
            l, x, norm_mix[l][None, :], w_in, dup(q_norm[l]), dup(k_norm[l]), c, s1, s2, ones_blockdiag)
        x = _mix_ffn_call(l, sinks[l], x, q, k, v, a, ga, gb, conv_w[l], conv_b[l][None, :],
                          conv_ln_g[l][None, :], conv_ln_b[l][None, :], w_conv_out, w_out,
                          norm_ffn[l][None, :], w_gate_up, w_down)
    return x
```
